```python
import math
import jax
import jax.numpy as jnp
from jax import lax
import numpy as np

D_MODEL = 1024
BATCH = 16
SEQ = 4096
DEPTH = 2

DIFF_HEADS = 4
DIFF_HEAD_DIM = 64
DSA_HEADS = 4
DSA_HEAD_DIM = 128
IDX_HEADS = 8
IDX_HEAD_DIM = 64
DSA_TOPK_MAX = 256
SSM_HEADS = 16
SSM_HEAD_DIM = 64
SSM_GROUPS = 2
SSM_STATE = 128
SSM_CONV = 4
SSM_CHUNK = 128
SSM_INNER = SSM_HEADS * SSM_HEAD_DIM
SSM_CONV_DIM = SSM_INNER + 2 * SSM_GROUPS * SSM_STATE
MOBA_HEADS = 8
MOBA_HEAD_DIM = 64
MOBA_BLOCK = 256
MOBA_TOPK = 3
MOBA_Q_CHUNK = 16
D_FF = ((8 * D_MODEL + 3 * 256 - 1) // (3 * 256)) * 256

ROPE_THETA = 10000.0
Q_BLOCK = 128
RMS_EPS = 1e-6

A_QK = 2 * DIFF_HEADS * DIFF_HEAD_DIM
A_V = DIFF_HEADS * 2 * DIFF_HEAD_DIM
B_Q = DSA_HEADS * DSA_HEAD_DIM
I_Q = IDX_HEADS * IDX_HEAD_DIM
EVEN_WIDTHS = (A_QK, A_QK, A_V, B_Q, DSA_HEAD_DIM, DSA_HEAD_DIM, I_Q, IDX_HEAD_DIM, IDX_HEADS)
EVEN_IN = sum(EVEN_WIDTHS)
EVEN_MIX = A_V + B_Q
M_QKV = MOBA_HEADS * MOBA_HEAD_DIM
ODD_WIDTHS = (SSM_INNER, SSM_CONV_DIM, SSM_HEADS, M_QKV, M_QKV, M_QKV)
ODD_IN = sum(ODD_WIDTHS)
ODD_MIX = SSM_INNER + M_QKV

kernel_name = 'hybrid_diff_dsa_ssd_moba_block'


def _split(t, widths):
    outs, start = [], 0
    for w in widths:
        outs.append(t[..., start:start + w])
        start += w
    return outs


def rms_norm(x, g):
    xf = x.astype(jnp.float32)
    y = xf * lax.rsqrt(jnp.mean(xf * xf, axis=-1, keepdims=True) + RMS_EPS)
    return (y * g.astype(jnp.float32)).astype(x.dtype)


def rope_tables(seq, dim):
    inv = ROPE_THETA ** (-jnp.arange(0, dim, 2, dtype=jnp.float32) / dim)
    ang = jnp.arange(seq, dtype=jnp.float32)[:, None] * inv[None, :]
    return jnp.cos(ang), jnp.sin(ang)


def apply_rope(t, cos, sin):
    half = t.shape[-1] // 2
    tf = t.astype(jnp.float32)
    t1, t2 = tf[..., :half], tf[..., half:]
    c = cos[None, :, None, :]
    s = sin[None, :, None, :]
    return jnp.concatenate([t1 * c - t2 * s, t2 * c + t1 * s], axis=-1).astype(t.dtype)


def diff_attention(q, k, v, lam, lam_init, subln_g):
    bsz, seq, h2, d = q.shape
    heads = h2 // 2
    scale = d ** -0.5
    kpos = jnp.arange(seq)

    def block(i):
        start = i * Q_BLOCK
        qb = lax.dynamic_slice_in_dim(q, start, Q_BLOCK, axis=1)
        logits = jnp.einsum('bqhd,bkhd->bhqk', qb, k).astype(jnp.float32) * scale
        qpos = start + jnp.arange(Q_BLOCK)
        logits = jnp.where(kpos[None, :] <= qpos[:, None], logits, -jnp.inf)
        p = jax.nn.softmax(logits, axis=-1).reshape(bsz, heads, 2, Q_BLOCK, seq)
        attn = p[:, :, 0] - lam * p[:, :, 1]
        return jnp.einsum('bhqk,bkhe->bqhe', attn.astype(v.dtype), v)

    out = lax.map(block, jnp.arange(seq // Q_BLOCK))
    out = out.transpose(1, 0, 2, 3, 4).reshape(bsz, seq, heads, 2 * d)
    out = rms_norm(out, subln_g) * (1.0 - lam_init)
    return out.reshape(bsz, seq, heads * 2 * d)


def dsa_attention(q, k, v, qi, ki, wi, topk):
    bsz, seq, hq, dh = q.shape
    scale = dh ** -0.5
    kpos = jnp.arange(seq)
    wi = wi.astype(jnp.float32) * (IDX_HEADS ** -0.5 * IDX_HEAD_DIM ** -0.5)
    take = jax.vmap(lambda t, idx: t[idx])

    def block(i):
        start = i * Q_BLOCK
        qpos = start + jnp.arange(Q_BLOCK)
        qib = lax.dynamic_slice_in_dim(qi, start, Q_BLOCK, axis=1)
        wib = lax.dynamic_slice_in_dim(wi, start, Q_BLOCK, axis=1)
        rel = jax.nn.relu(jnp.einsum('bqhd,bkd->bqhk', qib, ki).astype(jnp.float32))
        score = jnp.einsum('bqhk,bqh->bqk', rel, wib)
        score = jnp.where(kpos[None, None, :] <= qpos[None, :, None], score, -jnp.inf)
        _, idx = lax.top_k(score, topk)
        valid = idx <= qpos[None, :, None]
        ks = take(k, idx)
        vs = take(v, idx)
        qb = lax.dynamic_slice_in_dim(q, start, Q_BLOCK, axis=1)
        logits = jnp.einsum('bqhd,bqkd->bhqk', qb, ks).astype(jnp.float32) * scale
        logits = jnp.where(valid[:, None], logits, -jnp.inf)
        p = jax.nn.softmax(logits, axis=-1)
        return jnp.einsum('bhqk,bqkd->bqhd', p.astype(v.dtype), vs)

    out = lax.map(block, jnp.arange(seq // Q_BLOCK))
    return out.transpose(1, 0, 2, 3, 4).reshape(bsz, seq, hq * dh)


def ssd_scan(x, dt, a, bm, cm):
    bsz, seq, heads, hdim = x.shape
    groups, nstate = bm.shape[2], bm.shape[3]
    rep = heads // groups
    nch = seq // SSM_CHUNK
    f32 = jnp.float32
    xdt = x.astype(f32) * dt[..., None]

    def chunks(t):
        return t.reshape(bsz, nch, SSM_CHUNK, *t.shape[2:]).swapaxes(0, 1)

    tri = jnp.tril(jnp.ones((SSM_CHUNK, SSM_CHUNK), dtype=bool))

    def step(state, inp):
        xc, dac, bg, cg = inp
        bc = jnp.repeat(bg, rep, axis=2)
        cc = jnp.repeat(cg, rep, axis=2)
        acum = jnp.cumsum(dac, axis=1)
        seg = acum[:, :, None, :] - acum[:, None, :, :]
        decay = jnp.exp(jnp.where(tri[None, :, :, None], seg, -jnp.inf))
        scores = jnp.einsum('bthn,bshn->btsh', cc, bc) * decay
        y = jnp.einsum('btsh,bshp->bthp', scores, xc)
        y = y + jnp.einsum('bthn,bhpn->bthp', cc, state) * jnp.exp(acum)[..., None]
        tail = jnp.exp(acum[:, -1:, :] - acum)
        state = (state * jnp.exp(acum[:, -1, :])[:, :, None, None]
                 + jnp.einsum('bshn,bshp->bhpn', bc * tail[..., None], xc))
        return state, y

    state0 = jnp.zeros((bsz, heads, hdim, nstate), f32)
    _, ys = lax.scan(step, state0, (chunks(xdt), chunks(dt * a), chunks(bm.astype(f32)), chunks(cm.astype(f32))))
    return ys.swapaxes(0, 1).reshape(bsz, seq, heads, hdim)


def moba_attention(q, k, v):
    bsz, seq, heads, d = q.shape
    nblk = -(-seq // MOBA_BLOCK)
    pad = nblk * MOBA_BLOCK - seq

    def blocks(t):
        t = jnp.pad(t, ((0, 0), (0, pad), (0, 0), (0, 0)))
        return t.reshape(bsz, nblk, MOBA_BLOCK, heads, d).transpose(0, 3, 1, 2, 4)

    kb, vb = blocks(k), blocks(v)
    kmean = jnp.mean(kb.astype(jnp.float32), axis=3)
    nsel = min(MOBA_TOPK, nblk)
    scale = d ** -0.5
    blk_ids = jnp.arange(nblk)
    in_blk = jnp.arange(MOBA_BLOCK)
    take = jax.vmap(jax.vmap(lambda t, idx: t[idx]))

    def chunk(i):
        start = i * MOBA_Q_CHUNK
        own = start // MOBA_BLOCK
        qpos = start + jnp.arange(MOBA_Q_CHUNK)
        qc = lax.dynamic_slice_in_dim(q, start, MOBA_Q_CHUNK, axis=1).transpose(0, 2, 1, 3)
        gate = jnp.einsum('bhqd,bhnd->bhqn', qc.astype(jnp.float32), kmean)
        gate = jnp.where(blk_ids < own, gate, -jnp.inf)
        _, sel = lax.top_k(gate, nsel)
        valid = sel < own
        ks = take(kb, sel)
        vs = take(vb, sel)
        l_sel = jnp.einsum('bhqd,bhqnkd->bhqnk', qc, ks).astype(jnp.float32) * scale
        l_sel = jnp.where(valid[..., None], l_sel, -jnp.inf).reshape(bsz, heads, MOBA_Q_CHUNK, nsel * MOBA_BLOCK)
        k_own = lax.dynamic_index_in_dim(kb, own, axis=2, keepdims=False)
        v_own = lax.dynamic_index_in_dim(vb, own, axis=2, keepdims=False)
        l_own = jnp.einsum('bhqd,bhkd->bhqk', qc, k_own).astype(jnp.float32) * scale
        l_own = jnp.where(own * MOBA_BLOCK + in_blk[None, :] <= qpos[:, None], l_own, -jnp.inf)
        p = jax.nn.softmax(jnp.concatenate([l_sel, l_own], axis=-1), axis=-1).astype(v.dtype)
        p_sel = p[..., :nsel * MOBA_BLOCK].reshape(bsz, heads, MOBA_Q_CHUNK, nsel, MOBA_BLOCK)
        p_own = p[..., nsel * MOBA_BLOCK:]
        return (jnp.einsum('bhqnk,bhqnkd->bqhd', p_sel, vs)
                + jnp.einsum('bhqk,bhkd->bqhd', p_own, v_own))

    out = lax.map(chunk, jnp.arange(seq // MOBA_Q_CHUNK))
    return out.transpose(1, 0, 2, 3, 4).reshape(bsz, seq, heads * d)


def even_mixer(h, w_in, w_out, diff_lambda, diff_subln, lam_init, rope_a, rope_b, rope_i, dsa_topk):
    bsz, seq, _ = h.shape
    aq, ak, av, bq, bk, bv, iq, ik, iw = _split(h @ w_in, EVEN_WIDTHS)
    aq = apply_rope(aq.reshape(bsz, seq, 2 * DIFF_HEADS, DIFF_HEAD_DIM), *rope_a)
    ak = apply_rope(ak.reshape(bsz, seq, 2 * DIFF_HEADS, DIFF_HEAD_DIM), *rope_a)
    av = av.reshape(bsz, seq, DIFF_HEADS, 2 * DIFF_HEAD_DIM)
    lf = diff_lambda.astype(jnp.float32)
    lam = jnp.exp(jnp.sum(lf[0] * lf[1])) - jnp.exp(jnp.sum(lf[2] * lf[3])) + lam_init
    a_out = diff_attention(aq, ak, av, lam, lam_init, diff_subln)
    bq = apply_rope(bq.reshape(bsz, seq, DSA_HEADS, DSA_HEAD_DIM), *rope_b)
    bk = apply_rope(bk[:, :, None, :], *rope_b)[:, :, 0]
    iq = apply_rope(iq.reshape(bsz, seq, IDX_HEADS, IDX_HEAD_DIM), *rope_i)
    ik = apply_rope(ik[:, :, None, :], *rope_i)[:, :, 0]
    b_out = dsa_attention(bq, bk, bv, iq, ik, iw, dsa_topk)
    return jnp.concatenate([a_out, b_out], axis=-1) @ w_out


def odd_mixer(h, w_in, w_out, conv_w, conv_b, dt_bias, a_log, d_skip, norm_g, rope_m):
    bsz, seq, _ = h.shape
    f32 = jnp.float32
    z, xbc, dt_raw, mq, mk, mv = _split(h @ w_in, ODD_WIDTHS)
    xbc = lax.conv_general_dilated(xbc, conv_w[:, None, :], window_strides=(1,),
                                   padding=[(SSM_CONV - 1, 0)],
                                   dimension_numbers=('NWC', 'WIO', 'NWC'),
                                   feature_group_count=SSM_CONV_DIM)
    xbc = jax.nn.silu(xbc + conv_b)
    xs, bm, cm = _split(xbc, (SSM_INNER, SSM_GROUPS * SSM_STATE, SSM_GROUPS * SSM_STATE))
    xs = xs.reshape(bsz, seq, SSM_HEADS, SSM_HEAD_DIM)
    bm = bm.reshape(bsz, seq, SSM_GROUPS, SSM_STATE)
    cm = cm.reshape(bsz, seq, SSM_GROUPS, SSM_STATE)
    dt = jax.nn.softplus(dt_raw.astype(f32) + dt_bias.astype(f32))
    a = -jnp.exp(a_log.astype(f32))
    y = ssd_scan(xs, dt, a, bm, cm) + d_skip.astype(f32)[:, None] * xs.astype(f32)
    y = y.reshape(bsz, seq, SSM_INNER).astype(h.dtype) * jax.nn.silu(z)
    y = rms_norm(y.reshape(bsz, seq, SSM_GROUPS, SSM_INNER // SSM_GROUPS),
                 norm_g.reshape(SSM_GROUPS, SSM_INNER // SSM_GROUPS)).reshape(bsz, seq, SSM_INNER)
    mq = apply_rope(mq.reshape(bsz, seq, MOBA_HEADS, MOBA_HEAD_DIM), *rope_m)
    mk = apply_rope(mk.reshape(bsz, seq, MOBA_HEADS, MOBA_HEAD_DIM), *rope_m)
    mv = mv.reshape(bsz, seq, MOBA_HEADS, MOBA_HEAD_DIM)
    m_out = moba_attention(mq, mk, mv)
    return jnp.concatenate([y, m_out], axis=-1) @ w_out


def swiglu(h, w_gate, w_up, w_down):
    return (jax.nn.silu(h @ w_gate) * (h @ w_up)) @ w_down


def setup_inputs(seed: int = 0) -> dict:
    key = jax.random.key(seed)
    k = jax.random.split(key, 20)
    f32 = jnp.float32
    ne = (DEPTH + 1) // 2
    no = DEPTH // 2

    def dense(kk, shape, fan_in):
        return jax.random.normal(kk, shape, f32) * fan_in ** -0.5

    def gain(kk, shape):
        return 1.0 + 0.02 * jax.random.normal(kk, shape, f32)

    dt0 = jnp.exp(jax.random.uniform(k[18], (no, SSM_HEADS), f32, math.log(1e-3), math.log(1e-1)))
    return {
        'x': jax.random.normal(k[0], (BATCH, SEQ, D_MODEL), f32),
        'norm_mix_pre': gain(k[1], (DEPTH, D_MODEL)),
        'norm_mix_post': gain(k[2], (DEPTH, D_MODEL)),
        'norm_ffn_pre': gain(k[3], (DEPTH, D_MODEL)),
        'norm_ffn_post': gain(k[4], (DEPTH, D_MODEL)),
        'ffn_gate': dense(k[5], (DEPTH, D_MODEL, D_FF), D_MODEL),
        'ffn_up': dense(k[6], (DEPTH, D_MODEL, D_FF), D_MODEL),
        'ffn_down': dense(k[7], (DEPTH, D_FF, D_MODEL), D_FF),
        'even_w_in': dense(k[8], (ne, D_MODEL, EVEN_IN), D_MODEL),
        'even_w_out': dense(k[9], (ne, EVEN_MIX, D_MODEL), EVEN_MIX),
        'diff_lambda': 0.1 * jax.random.normal(k[10], (ne, 4, DIFF_HEAD_DIM), f32),
        'diff_subln': gain(k[11], (ne, 2 * DIFF_HEAD_DIM)),
        'odd_w_in': dense(k[12], (no, D_MODEL, ODD_IN), D_MODEL),
        'odd_w_out': dense(k[13], (no, ODD_MIX, D_MODEL), ODD_MIX),
        'ssm_conv_w': dense(k[14], (no, SSM_CONV, SSM_CONV_DIM), SSM_CONV),
        'ssm_conv_b': 0.02 * jax.random.normal(k[15], (no, SSM_CONV_DIM), f32),
        'ssm_dt_bias': dt0 + jnp.log(-jnp.expm1(-dt0)),
        'ssm_a_log': jnp.log(jax.random.uniform(k[16], (no, SSM_HEADS), f32, 1.0, 16.0)),
        'ssm_d': 1.0 + 0.1 * jax.random.normal(k[17], (no, SSM_HEADS), f32),
        'ssm_norm': gain(k[19], (no, SSM_INNER)),
    }


def reference(x, norm_mix_pre, norm_mix_post, norm_ffn_pre, norm_ffn_post, ffn_gate, ffn_up, ffn_down,
              even_w_in, even_w_out, diff_lambda, diff_subln, odd_w_in, odd_w_out, ssm_conv_w,
              ssm_conv_b, ssm_dt_bias, ssm_a_log, ssm_d, ssm_norm):
    seq = x.shape[1]
    rope_a = rope_tables(seq, DIFF_HEAD_DIM)
    rope_b = rope_tables(seq, DSA_HEAD_DIM)
    rope_i = rope_tables(seq, IDX_HEAD_DIM)
    rope_m = rope_tables(seq, MOBA_HEAD_DIM)
    dsa_topk = min(DSA_TOPK_MAX, seq // 4)
    h = x
    for i in range(DEPTH):
        j = i // 2
        hn = rms_norm(h, norm_mix_pre[i])
        if i % 2 == 0:
            lam_init = 0.8 - 0.6 * math.exp(-0.3 * i)
            mix = even_mixer(hn, even_w_in[j], even_w_out[j], diff_lambda[j], diff_subln[j], lam_init,
                             rope_a, rope_b, rope_i, dsa_topk)
        else:
            mix = odd_mixer(hn, odd_w_in[j], odd_w_out[j], ssm_conv_w[j], ssm_conv_b[j], ssm_dt_bias[j],
                            ssm_a_log[j], ssm_d[j], ssm_norm[j], rope_m)
        h = h + rms_norm(mix, norm_mix_post[i])
        hn = rms_norm(h, norm_ffn_pre[i])
        h = h + rms_norm(swiglu(hn, ffn_gate[i], ffn_up[i], ffn_down[i]), norm_ffn_post[i])
    return h
```

```python
import functools
import math

import jax
import jax.numpy as jnp
import numpy as np
from jax import lax
from jax.experimental import pallas as pl
from jax.experimental.pallas import tpu as pltpu

F32 = jnp.float32
BF16 = jnp.bfloat16

D_MODEL = 1024
DIFF_HEADS = 4
DIFF_HEAD_DIM = 64
DSA_HEADS = 4
DSA_HEAD_DIM = 128
IDX_HEADS = 8
IDX_HEAD_DIM = 64
DSA_TOPK_MAX = 256
SSM_HEADS = 16
SSM_HEAD_DIM = 64
SSM_GROUPS = 2
SSM_STATE = 128
SSM_CONV = 4
SSM_INNER = SSM_HEADS * SSM_HEAD_DIM
SSM_CONV_DIM = SSM_INNER + 2 * SSM_GROUPS * SSM_STATE
MOBA_HEADS = 8
MOBA_HEAD_DIM = 64
MOBA_BLOCK = 256
MOBA_TOPK = 3
D_FF = 2816
ROPE_THETA = 10000.0
RMS_EPS = 1e-6

LANES = 128
VMEM_LIMIT = 56 * 1024 * 1024
NEG = -1e30
INT_MIN = -(2 ** 31)

A_QK = 2 * DIFF_HEADS * DIFF_HEAD_DIM
A_V = DIFF_HEADS * 2 * DIFF_HEAD_DIM
B_Q = DSA_HEADS * DSA_HEAD_DIM
I_Q = IDX_HEADS * IDX_HEAD_DIM
M_QKV = MOBA_HEADS * MOBA_HEAD_DIM

PLAIN, ROPE64, ROPE128 = 0, 1, 2


def _cparams(sem):
    return pltpu.CompilerParams(dimension_semantics=sem, vmem_limit_bytes=VMEM_LIMIT)


def _rms(x, g):
    return x * lax.rsqrt(jnp.mean(x * x, axis=-1, keepdims=True) + RMS_EPS) * g


def _dot(a, b):
    return jnp.dot(a, b, preferred_element_type=F32)


def _dot_nt(a, b):
    return lax.dot_general(a, b, (((1,), (1,)), ((), ())), preferred_element_type=F32)


def _split3(x):
    hi = x.astype(BF16)
    r1 = x - hi.astype(F32)
    mid = r1.astype(BF16)
    lo = (r1 - mid.astype(F32)).astype(BF16)
    return hi, mid, lo


def _dot3(x, m):
    hi, mid, lo = _split3(x)
    return _dot(hi, m) + _dot(mid, m) + _dot(lo, m)


def _inproj_kernel(x_ref, g_ref, w_ref, cos_ref, sin_ref, *out_refs, groups):
    xn = _rms(x_ref[...], g_ref[...]).astype(BF16)
    for (start, width, kind), o_ref in zip(groups, out_refs):
        y = _dot(xn, w_ref[:, start:start + width])
        if kind == PLAIN:
            o_ref[...] = y.astype(o_ref.dtype)
        else:
            cos = cos_ref[kind - 1]
            sin = sin_ref[kind - 1]
            for c in range(0, width, LANES):
                yb = y[:, c:c + LANES]
                o_ref[:, c:c + LANES] = (yb * cos + pltpu.roll(yb, 64, 1) * sin).astype(o_ref.dtype)


def _inproj(h, g, w, cos_t, sin_t, groups, out_dtypes, seq, tm=512):
    t, d = h.shape
    n = w.shape[1]
    per_seq = seq // tm
    kern = functools.partial(_inproj_kernel, groups=groups)
    out_shape = [jax.ShapeDtypeStruct((t, wd), dt) for (_, wd, _), dt in zip(groups, out_dtypes)]
    out_specs = [pl.BlockSpec((tm, wd), lambda i: (i, 0)) for (_, wd, _) in groups]
    return pl.pallas_call(
        kern,
        grid=(t // tm,),
        in_specs=[
            pl.BlockSpec((tm, d), lambda i: (i, 0)),
            pl.BlockSpec((1, d), lambda i: (0, 0)),
            pl.BlockSpec((d, n), lambda i: (0, 0)),
            pl.BlockSpec((2, tm, LANES), lambda i: (0, i % per_seq, 0)),
            pl.BlockSpec((2, tm, LANES), lambda i: (0, i % per_seq, 0)),
        ],
        out_specs=out_specs,
        out_shape=out_shape,
        compiler_params=_cparams(("parallel",)),
        name="inproj",
    )(h, g.reshape(1, d), w, cos_t, sin_t)


def _outproj_kernel(h_ref, a_ref, b_ref, wa_ref, wb_ref, g_ref, o_ref):
    y = _dot(a_ref[...], wa_ref[...]) + _dot(b_ref[...], wb_ref[...])
    o_ref[...] = h_ref[...] + _rms(y, g_ref[...])


def _outproj(h, a, b, wa, wb, g, tm=512):
    t, d = h.shape
    ka, kb = a.shape[1], b.shape[1]
    return pl.pallas_call(
        _outproj_kernel,
        grid=(t // tm,),
        in_specs=[
            pl.BlockSpec((tm, d), lambda i: (i, 0)),
            pl.BlockSpec((tm, ka), lambda i: (i, 0)),
            pl.BlockSpec((tm, kb), lambda i: (i, 0)),
            pl.BlockSpec((ka, d), lambda i: (0, 0)),
            pl.BlockSpec((kb, d), lambda i: (0, 0)),
            pl.BlockSpec((1, d), lambda i: (0, 0)),
        ],
        out_specs=pl.BlockSpec((tm, d), lambda i: (i, 0)),
        out_shape=jax.ShapeDtypeStruct((t, d), F32),
        compiler_params=_cparams(("parallel",)),
        name="outproj",
    )(h, a, b, wa, wb, g.reshape(1, d))


def _ffn_kernel(h_ref, gpre_ref, wg_ref, wu_ref, wd_ref, gpost_ref, o_ref, acc_ref, *, fc):
    h = h_ref[...]
    hn = _rms(h, gpre_ref[...]).astype(BF16)
    dff = wg_ref.shape[1]
    for c in range(0, dff, fc):
        gt = _dot(hn, wg_ref[:, c:c + fc])
        up = _dot(hn, wu_ref[:, c:c + fc])
        act = (gt * jax.nn.sigmoid(gt) * up).astype(BF16)
        part = _dot(act, wd_ref[c:c + fc, :])
        if c == 0:
            acc_ref[...] = part
        else:
            acc_ref[...] += part
    o_ref[...] = h + _rms(acc_ref[...], gpost_ref[...])


def _ffn(h, gpre, wg, wu, wd, gpost, tm=512, fc=256):
    t, d = h.shape
    dff = wg.shape[1]
    const = dict(pipeline_mode=pl.Buffered(1))
    return pl.pallas_call(
        functools.partial(_ffn_kernel, fc=fc),
        grid=(t // tm,),
        in_specs=[
            pl.BlockSpec((tm, d), lambda i: (i, 0)),
            pl.BlockSpec((1, d), lambda i: (0, 0)),
            pl.BlockSpec((d, dff), lambda i: (0, 0), **const),
            pl.BlockSpec((d, dff), lambda i: (0, 0), **const),
            pl.BlockSpec((dff, d), lambda i: (0, 0), **const),
            pl.BlockSpec((1, d), lambda i: (0, 0)),
        ],
        out_specs=pl.BlockSpec((tm, d), lambda i: (i, 0)),
        out_shape=jax.ShapeDtypeStruct((t, d), F32),
        scratch_shapes=[pltpu.VMEM((tm, d), F32)],
        compiler_params=_cparams(("parallel",)),
        name="ffn",
    )(h, gpre.reshape(1, d), wg, wu, wd, gpost.reshape(1, d))


def _online(s, vc, m, l, acc):
    m_new = jnp.maximum(m, jnp.max(s, axis=-1, keepdims=True))
    alpha = jnp.exp(m - m_new)
    p = jnp.exp(s - m_new)
    l = alpha * l + jnp.sum(p, axis=-1, keepdims=True)
    acc = alpha * acc + _dot(p.astype(BF16), vc)
    return m_new, l, acc


def _pair_masks(rows):
    lane = lax.broadcasted_iota(jnp.int32, (rows, LANES), 1)
    is_a = (lane % 64) < 32
    return is_a


def _diff_kernel(lam_ref, q_ref, k_ref, v_ref, g_ref, o_ref, *, tq, out_scale):
    qi = pl.program_id(2)
    q = q_ref[...]
    is_a = _pair_masks(tq)
    zero = jnp.zeros_like(q)
    qs = (jnp.where(is_a, q, zero), jnp.where(is_a, zero, q))
    scale = DIFF_HEAD_DIM ** -0.5
    row = lax.broadcasted_iota(jnp.int32, (tq, tq), 0)
    col = lax.broadcasted_iota(jnp.int32, (tq, tq), 1)
    causal = col <= row

    def step(j, carry, diag):
        off = pl.multiple_of(j * tq, tq)
        kc = k_ref[pl.ds(off, tq), :]
        vc = v_ref[pl.ds(off, tq), :]
        out = []
        for mp in range(2):
            m, l, acc = carry[3 * mp:3 * mp + 3]
            s = _dot_nt(qs[mp], kc) * scale
            if diag:
                s = jnp.where(causal, s, NEG)
            out.extend(_online(s, vc, m, l, acc))
        return tuple(out)

    m0 = jnp.full((tq, 1), NEG, F32)
    l0 = jnp.zeros((tq, 1), F32)
    a0 = jnp.zeros((tq, LANES), F32)
    carry = (m0, l0, a0, m0, l0, a0)
    carry = lax.fori_loop(0, qi, lambda j, c: step(j, c, False), carry)
    _, l_a, acc_a, _, l_b, acc_b = step(qi, carry, True)
    o = acc_a / l_a - lam_ref[0] * (acc_b / l_b)
    o_ref[...] = (_rms(o, g_ref[...]) * out_scale).astype(o_ref.dtype)


def _diff_attention(aq, ak, av, lam, subln_g, lam_init, tq=256):
    bsz, seq, _ = aq.shape
    kern = functools.partial(_diff_kernel, tq=tq, out_scale=1.0 - lam_init)
    return pl.pallas_call(
        kern,
        grid=(bsz, DIFF_HEADS, seq // tq),
        in_specs=[
            pl.BlockSpec(memory_space=pltpu.SMEM),
            pl.BlockSpec((None, tq, LANES), lambda b, h, i: (b, i, h)),
            pl.BlockSpec((None, seq, LANES), lambda b, h, i: (b, 0, h)),
            pl.BlockSpec((None, seq, LANES), lambda b, h, i: (b, 0, h)),
            pl.BlockSpec((1, LANES), lambda b, h, i: (0, 0)),
        ],
        out_specs=pl.BlockSpec((None, tq, LANES), lambda b, h, i: (b, i, h)),
        out_shape=jax.ShapeDtypeStruct((bsz, seq, A_V), BF16),
        compiler_params=_cparams(("parallel", "parallel", "arbitrary")),
        name="diff_attn",
    )(lam.reshape(1), aq, ak, av, subln_g.reshape(1, LANES))


def _dsa_kernel(q_ref, k_ref, v_ref, iq_ref, ik_ref, iw_ref, o_ref, key_sc, m_sc, l_sc, acc_sc,
                *, tq, topk):
    qi = pl.program_id(1)
    nch = qi + 1
    row = lax.broadcasted_iota(jnp.int32, (tq, tq), 0)
    col = lax.broadcasted_iota(jnp.int32, (tq, tq), 1)
    causal = col <= row
    ones = jnp.ones((tq, LANES), BF16)
    min_key = jnp.int32(INT_MIN)

    is_a = _pair_masks(tq)
    w = iw_ref[...] * (IDX_HEADS ** -0.5 * IDX_HEAD_DIM ** -0.5)
    qms, wcols = [], []
    for hh in range(IDX_HEADS):
        blk = iq_ref[:, (hh // 2) * LANES:(hh // 2 + 1) * LANES]
        zero = jnp.zeros_like(blk)
        qms.append(jnp.where(is_a, blk, zero) if hh % 2 == 0 else jnp.where(is_a, zero, blk))
        wcols.append(w[:, hh:hh + 1])

    def score_chunk(j, diag):
        off = pl.multiple_of(j * tq, tq)
        kc = ik_ref[pl.ds(off, tq), :]
        sc = jnp.zeros((tq, tq), F32)
        for hh in range(IDX_HEADS):
            sc = sc + wcols[hh] * jnp.maximum(_dot_nt(qms[hh], kc), 0.0)
        bits = pltpu.bitcast(sc, jnp.int32)
        key = bits ^ ((bits >> 31) & jnp.int32(0x7FFFFFFF))
        if diag:
            key = jnp.where(causal, key, min_key)
        key_sc[j] = key

    def score_body(j, c):
        score_chunk(j, False)
        return c

    lax.fori_loop(0, qi, score_body, 0)
    score_chunk(qi, True)

    def count(pred_fn):
        def body(j, cnt):
            msk = jnp.where(pred_fn(key_sc[j], j), 1.0, 0.0).astype(BF16)
            return cnt + _dot(msk, ones)
        return lax.fori_loop(0, nch, body, jnp.zeros((tq, LANES), F32))[:, :1]

    kf = float(topk)
    c0 = count(lambda key, j: key >= 0)
    thr = jnp.where(c0 >= kf, jnp.int32(0), min_key)

    def bit_body(bi, thr):
        cand = thr | (jnp.int32(1) << (30 - bi))
        cnt = count(lambda key, j: key >= cand)
        return jnp.where(cnt >= kf, cand, thr)

    thr = lax.fori_loop(0, 31, bit_body, thr)
    thr = jnp.maximum(thr, min_key + 1)

    cnt_ge = count(lambda key, j: key >= thr)

    @pl.when(jnp.max(cnt_ge) > kf)
    def _():
        cnt_gt = count(lambda key, j: key > thr)
        need = kf - cnt_gt

        def pos_body(bi, cut):
            cand = cut | (jnp.int32(1) << (30 - bi))
            cnt = count(lambda key, j: (key == thr) & (col + j * tq < cand))
            return jnp.where(cnt < need, cand, cut)

        cut = lax.fori_loop(0, 31, pos_body, jnp.zeros((tq, 1), jnp.int32))

        def demote(j, c):
            key = key_sc[j]
            key_sc[j] = jnp.where((key == thr) & (col + j * tq > cut), thr - 1, key)
            return c

        lax.fori_loop(0, nch, demote, 0)

    scale = DSA_HEAD_DIM ** -0.5
    m_sc[...] = jnp.full(m_sc.shape, NEG, F32)
    l_sc[...] = jnp.zeros(l_sc.shape, F32)
    acc_sc[...] = jnp.zeros(acc_sc.shape, F32)

    def att_body(j, c):
        off = pl.multiple_of(j * tq, tq)
        kc = k_ref[pl.ds(off, tq), :]
        vc = v_ref[pl.ds(off, tq), :]
        sel = key_sc[j] >= thr
        for hh in range(DSA_HEADS):
            s = _dot_nt(q_ref[:, hh * LANES:(hh + 1) * LANES], kc) * scale
            s = jnp.where(sel, s, NEG)
            m, l, acc = _online(s, vc, m_sc[hh], l_sc[hh], acc_sc[hh])
            m_sc[hh] = m
            l_sc[hh] = l
            acc_sc[hh] = acc
        return c

    lax.fori_loop(0, nch, att_body, 0)
    for hh in range(DSA_HEADS):
        o_ref[:, hh * LANES:(hh + 1) * LANES] = (acc_sc[hh] / l_sc[hh]).astype(o_ref.dtype)


def _dsa_attention(bq, bk, bv, iq, ik, iw, topk, tq=256):
    bsz, seq, _ = bq.shape
    kern = functools.partial(_dsa_kernel, tq=tq, topk=topk)
    full = lambda b, i: (b, 0, 0)
    tile = lambda b, i: (b, i, 0)
    return pl.pallas_call(
        kern,
        grid=(bsz, seq // tq),
        in_specs=[
            pl.BlockSpec((None, tq, B_Q), tile),
            pl.BlockSpec((None, seq, LANES), full),
            pl.BlockSpec((None, seq, LANES), full),
            pl.BlockSpec((None, tq, I_Q), tile),
            pl.BlockSpec((None, seq, LANES), full),
            pl.BlockSpec((None, tq, LANES), tile),
        ],
        out_specs=pl.BlockSpec((None, tq, B_Q), tile),
        out_shape=jax.ShapeDtypeStruct((bsz, seq, B_Q), BF16),
        scratch_shapes=[
            pltpu.VMEM((seq // tq, tq, tq), jnp.int32),
            pltpu.VMEM((DSA_HEADS, tq, 1), F32),
            pltpu.VMEM((DSA_HEADS, tq, 1), F32),
            pltpu.VMEM((DSA_HEADS, tq, LANES), F32),
        ],
        compiler_params=_cparams(("parallel", "arbitrary")),
        name="dsa",
    )(bq, bk, bv, iq, ik, iw)


def _ssd_kernel(xbc_ref, z_ref, dt_ref, cw_ref, cb_ref, dtb_ref, alog_ref, dskip_ref, ng_ref,
                e_pair_ref, e_full_ref, o_ref, xext_sc, state_sc, *, lc):
    ci = pl.program_id(1)
    npair = SSM_HEADS // 2
    gw = SSM_INNER // SSM_GROUPS

    @pl.when(ci == 0)
    def _():
        xext_sc[0:8, :] = jnp.zeros((8, SSM_CONV_DIM), F32)
        state_sc[...] = jnp.zeros(state_sc.shape, F32)

    xext_sc[8:8 + lc, :] = xbc_ref[...].astype(F32)
    conv = cb_ref[...] + cw_ref[SSM_CONV - 1:SSM_CONV, :] * xext_sc[8:8 + lc, :]
    for sh in range(1, SSM_CONV):
        conv = conv + cw_ref[SSM_CONV - 1 - sh:SSM_CONV - sh, :] * xext_sc[8 - sh:8 - sh + lc, :]
    xext_sc[0:8, :] = xext_sc[lc:lc + 8, :]
    xbc = conv * jax.nn.sigmoid(conv)
    xs = xbc[:, :SSM_INNER]
    bm = [xbc[:, SSM_INNER + g * SSM_STATE:SSM_INNER + (g + 1) * SSM_STATE] for g in range(SSM_GROUPS)]
    cm = [xbc[:, SSM_INNER + (SSM_GROUPS + g) * SSM_STATE:SSM_INNER + (SSM_GROUPS + g + 1) * SSM_STATE]
          for g in range(SSM_GROUPS)]

    dtr = dt_ref[...] + dtb_ref[...]
    dt = jnp.maximum(dtr, 0.0) + jnp.log1p(jnp.exp(-jnp.abs(dtr)))
    da = dt * (-jnp.exp(alog_ref[...]))
    r_i = lax.broadcasted_iota(jnp.int32, (lc, lc), 0)
    c_i = lax.broadcasted_iota(jnp.int32, (lc, lc), 1)
    tri = c_i <= r_i
    tri_bf = jnp.where(tri, 1.0, 0.0).astype(BF16)
    hi, mid, lo = _split3(da)
    acum = _dot(tri_bf, hi) + _dot(tri_bf, mid) + _dot(tri_bf, lo)
    acum_t = acum.T
    tail_t = jnp.exp(acum_t[:, lc - 1:lc] - acum_t)

    e_pair = e_pair_ref[...]
    e_full = e_full_ref[...]
    dt_x = _dot3(dt, e_pair)
    acum_x = _dot3(acum, e_pair)
    alast_x = acum_x[lc - 1:lc, :]
    acum_cb = _dot3(acum, e_full)

    xdt = (xs * dt_x).astype(BF16)
    lane = lax.broadcasted_iota(jnp.int32, (lc, LANES), 1)
    first = lane < SSM_HEAD_DIM
    lane_n = lax.broadcasted_iota(jnp.int32, (SSM_STATE, LANES), 1)
    first_n = lane_n < SSM_HEAD_DIM

    ys = []
    for g in range(SSM_GROUPS):
        cg = cm[g].astype(BF16)
        bg = bm[g].astype(BF16)
        cb = _dot_nt(cg, bg)
        bm_t = bm[g].T
        for pp in range(g * npair // SSM_GROUPS, (g + 1) * npair // SSM_GROUPS):
            xp = xdt[:, pp * LANES:(pp + 1) * LANES]
            yh, sh_new = [], []
            for sub in range(2):
                hd = 2 * pp + sub
                seg = acum_cb[:, hd * LANES:(hd + 1) * LANES] - acum_t[hd:hd + 1, :]
                decay = jnp.exp(jnp.where(tri, seg, -jnp.inf))
                yh.append(_dot((cb * decay).astype(BF16), xp))
                sh_new.append(_dot((bm_t * tail_t[hd:hd + 1, :]).astype(BF16), xp))
            st = state_sc[pp]
            y_in = _dot(cg, st.astype(BF16)) * jnp.exp(acum_x[:, pp * LANES:(pp + 1) * LANES])
            ys.append(jnp.where(first, yh[0], yh[1]) + y_in)
            state_sc[pp] = (st * jnp.exp(alast_x[:, pp * LANES:(pp + 1) * LANES])
                            + jnp.where(first_n, sh_new[0], sh_new[1]))
    y = jnp.concatenate(ys, axis=-1) + dskip_ref[...] * xs
    zf = z_ref[...].astype(F32)
    y = y * (zf * jax.nn.sigmoid(zf))
    for g in range(SSM_GROUPS):
        sl = slice(g * gw, (g + 1) * gw)
        o_ref[:, sl] = _rms(y[:, sl], ng_ref[:, sl]).astype(o_ref.dtype)


def _ssd(xbc, z, dt_raw, conv_w, conv_b, dt_bias, a_log, d_skip, norm_g, lc=128):
    bsz, seq, _ = xbc.shape
    pad = lambda v: jnp.pad(v.astype(F32), (0, LANES - v.shape[0])).reshape(1, LANES)
    heads = np.arange(LANES)[:, None]
    e_pair = jnp.asarray(heads == (np.arange(SSM_INNER)[None, :] // SSM_HEAD_DIM), BF16)
    e_full = jnp.asarray(heads == (np.arange(SSM_HEADS * LANES)[None, :] // LANES), BF16)
    dskip_x = jnp.repeat(d_skip.astype(F32), SSM_HEAD_DIM).reshape(1, SSM_INNER)
    tile = lambda b, c: (b, c, 0)
    const = lambda b, c: (0, 0)
    return pl.pallas_call(
        functools.partial(_ssd_kernel, lc=lc),
        grid=(bsz, seq // lc),
        in_specs=[
            pl.BlockSpec((None, lc, SSM_CONV_DIM), tile),
            pl.BlockSpec((None, lc, SSM_INNER), tile),
            pl.BlockSpec((None, lc, LANES), tile),
            pl.BlockSpec((SSM_CONV, SSM_CONV_DIM), const),
            pl.BlockSpec((1, SSM_CONV_DIM), const),
            pl.BlockSpec((1, LANES), const),
            pl.BlockSpec((1, LANES), const),
            pl.BlockSpec((1, SSM_INNER), const),
            pl.BlockSpec((1, SSM_INNER), const),
            pl.BlockSpec((LANES, SSM_INNER), const),
            pl.BlockSpec((LANES, SSM_HEADS * LANES), const),
        ],
        out_specs=pl.BlockSpec((None, lc, SSM_INNER), tile),
        out_shape=jax.ShapeDtypeStruct((bsz, seq, SSM_INNER), BF16),
        scratch_shapes=[
            pltpu.VMEM((lc + 8, SSM_CONV_DIM), F32),
            pltpu.VMEM((SSM_HEADS // 2, SSM_STATE, LANES), F32),
        ],
        compiler_params=_cparams(("parallel", "arbitrary")),
        name="ssd",
    )(xbc, z, dt_raw, conv_w.astype(F32), conv_b.astype(F32).reshape(1, SSM_CONV_DIM), pad(dt_bias),
      pad(a_log), dskip_x, norm_g.astype(F32).reshape(1, SSM_INNER), e_pair, e_full)


def _moba_kernel(q_ref, k_ref, v_ref, o_ref, kmean_sc, *, nblk):
    qi = pl.program_id(2)
    tq = MOBA_BLOCK
    scale = MOBA_HEAD_DIM ** -0.5

    @pl.when(qi == 0)
    def _():
        kmean_sc[...] = jnp.zeros(kmean_sc.shape, F32)
        for j in range(nblk):
            kb = k_ref[j * tq:(j + 1) * tq, :].astype(F32)
            kmean_sc[j:j + 1, :] = jnp.sum(kb, axis=0, keepdims=True) * (1.0 / tq)

    q = q_ref[...]
    is_a = _pair_masks(tq)
    zero = jnp.zeros_like(q)
    qs = (jnp.where(is_a, q, zero), jnp.where(is_a, zero, q))
    kmean = kmean_sc[...].astype(BF16)
    lane = lax.broadcasted_iota(jnp.int32, (tq, LANES), 1)
    lane_f = lane.astype(F32)
    past = lane < qi
    row = lax.broadcasted_iota(jnp.int32, (tq, tq), 0)
    col = lax.broadcasted_iota(jnp.int32, (tq, tq), 1)
    causal = col <= row
    bc_r = lax.broadcasted_iota(jnp.int32, (LANES, LANES), 0)

    outs = []
    for sub in range(2):
        qm = qs[sub]
        gate = jnp.where(past, _dot_nt(qm, kmean), -jnp.inf)
        sel = jnp.zeros((tq, LANES), F32)
        for _ in range(MOBA_TOPK):
            mx = jnp.max(gate, axis=-1, keepdims=True)
            first = jnp.min(jnp.where(gate == mx, lane_f, float(LANES)), axis=-1, keepdims=True)
            hit = (lane_f == first) & (mx > -jnp.inf)
            sel = jnp.where(hit, 1.0, sel)
            gate = jnp.where(hit, -jnp.inf, gate)
        sel_bf = sel.astype(BF16)

        def step(j, carry, diag):
            off = pl.multiple_of(j * tq, tq)
            kc = k_ref[pl.ds(off, tq), :]
            vc = v_ref[pl.ds(off, tq), :]
            s = _dot_nt(qm, kc) * scale
            if diag:
                s = jnp.where(causal, s, NEG)
            else:
                pick = jnp.where(bc_r == j, 1.0, 0.0).astype(BF16)
                keep = _dot(sel_bf, pick)
                s = jnp.where(jnp.concatenate([keep, keep], axis=-1) > 0.5, s, NEG)
            return _online(s, vc, *carry)

        carry = (jnp.full((tq, 1), NEG, F32), jnp.zeros((tq, 1), F32), jnp.zeros((tq, LANES), F32))
        carry = lax.fori_loop(0, qi, lambda j, c: step(j, c, False), carry)
        _, l, acc = step(qi, carry, True)
        outs.append(acc / l)
    first_half = lane < MOBA_HEAD_DIM
    o_ref[...] = jnp.where(first_half, outs[0], outs[1]).astype(o_ref.dtype)


def _moba_attention(mq, mk, mv):
    bsz, seq, _ = mq.shape
    nblk = seq // MOBA_BLOCK
    tq = MOBA_BLOCK
    return pl.pallas_call(
        functools.partial(_moba_kernel, nblk=nblk),
        grid=(bsz, MOBA_HEADS // 2, nblk),
        in_specs=[
            pl.BlockSpec((None, tq, LANES), lambda b, p, i: (b, i, p)),
            pl.BlockSpec((None, seq, LANES), lambda b, p, i: (b, 0, p)),
            pl.BlockSpec((None, seq, LANES), lambda b, p, i: (b, 0, p)),
        ],
        out_specs=pl.BlockSpec((None, tq, LANES), lambda b, p, i: (b, i, p)),
        out_shape=jax.ShapeDtypeStruct((bsz, seq, M_QKV), BF16),
        scratch_shapes=[pltpu.VMEM((LANES, LANES), F32)],
        compiler_params=_cparams(("parallel", "parallel", "arbitrary")),
        name="moba",
    )(mq, mk, mv)


_PERM64 = np.concatenate([np.arange(0, 32), np.arange(64, 96), np.arange(32, 64), np.arange(96, 128)])
_DUP64 = np.concatenate([np.arange(0, 32), np.arange(0, 32), np.arange(32, 64), np.arange(32, 64)])


def _perm_blocks(start, width):
    return np.concatenate([start + c + _PERM64 for c in range(0, width, LANES)])


def _rope_tables(seq):
    def tab(dim):
        inv = ROPE_THETA ** (-jnp.arange(0, dim, 2, dtype=F32) / dim)
        ang = jnp.arange(seq, dtype=F32)[:, None] * inv[None, :]
        return jnp.cos(ang), jnp.sin(ang)
    c64, s64 = tab(64)
    c128, s128 = tab(128)
    cos_t = jnp.stack([jnp.tile(c64, (1, 4)), jnp.tile(c128, (1, 2))])
    sin_t = jnp.stack([jnp.concatenate([-s64, -s64, s64, s64], axis=1),
                       jnp.concatenate([-s128, s128], axis=1)])
    return cos_t, sin_t


def _even_layout(pad_col):
    o = np.cumsum([0, A_QK, A_QK, A_V, B_Q, DSA_HEAD_DIM, DSA_HEAD_DIM, I_Q, IDX_HEAD_DIM, IDX_HEADS])
    pad = np.full(LANES - IDX_HEADS, pad_col)
    return [
        (_perm_blocks(o[0], A_QK), ROPE64, BF16),
        (_perm_blocks(o[1], A_QK), ROPE64, BF16),
        (np.arange(o[2], o[3]), PLAIN, BF16),
        (np.arange(o[3], o[4]), ROPE128, BF16),
        (np.arange(o[4], o[5]), ROPE128, BF16),
        (np.arange(o[5], o[6]), PLAIN, BF16),
        (_perm_blocks(o[6], I_Q), ROPE64, BF16),
        (o[7] + _DUP64, ROPE64, BF16),
        (np.concatenate([np.arange(o[8], o[9]), pad]), PLAIN, F32),
    ]


def _odd_layout(pad_col):
    o = np.cumsum([0, SSM_INNER, SSM_CONV_DIM, SSM_HEADS, M_QKV, M_QKV, M_QKV])
    pad = np.full(LANES - SSM_HEADS, pad_col)
    return [
        (np.arange(o[0], o[1]), PLAIN, BF16),
        (np.arange(o[1], o[2]), PLAIN, BF16),
        (np.concatenate([np.arange(o[2], o[3]), pad]), PLAIN, F32),
        (_perm_blocks(o[3], M_QKV), ROPE64, BF16),
        (_perm_blocks(o[4], M_QKV), ROPE64, BF16),
        (np.arange(o[5], o[6]), PLAIN, BF16),
    ]


def _prep_in_weight(w, layout_fn):
    layout = layout_fn(w.shape[1])
    cols = np.concatenate([c for c, _, _ in layout])
    wl = jnp.concatenate([w, jnp.zeros((w.shape[0], 1), w.dtype)], axis=1)[:, cols].astype(BF16)
    widths = [len(c) for c, _, _ in layout]
    starts = np.concatenate([[0], np.cumsum(widths)[:-1]])
    groups = tuple((int(s), int(wd), int(k)) for s, wd, (_, k, _) in zip(starts, widths, layout))
    return wl, groups, [dt for _, _, dt in layout]


def kernel(x, norm_mix_pre, norm_mix_post, norm_ffn_pre, norm_ffn_post, ffn_gate, ffn_up, ffn_down,
           even_w_in, even_w_out, diff_lambda, diff_subln, odd_w_in, odd_w_out, ssm_conv_w,
           ssm_conv_b, ssm_dt_bias, ssm_a_log, ssm_d, ssm_norm):
    bsz, seq, d = x.shape
    depth = norm_mix_pre.shape[0]
    cos_t, sin_t = _rope_tables(seq)
    topk = min(DSA_TOPK_MAX, seq // 4)
    h = x.reshape(bsz * seq, d)
    r3 = lambda t: t.reshape(bsz, seq, t.shape[-1])
    for i in range(depth):
        j = i // 2
        if i % 2 == 0:
            w, groups, dts = _prep_in_weight(even_w_in[j], _even_layout)
            aq, ak, av, bq, bk, bv, iq, ik, iw = map(r3, _inproj(h, norm_mix_pre[i], w, cos_t, sin_t,
                                                                groups, dts, seq))
            lam_init = 0.8 - 0.6 * math.exp(-0.3 * i)
            lf = diff_lambda[j].astype(F32)
            lam = jnp.exp(jnp.sum(lf[0] * lf[1])) - jnp.exp(jnp.sum(lf[2] * lf[3])) + lam_init
            p1 = _diff_attention(aq, ak, av, lam, diff_subln[j].astype(F32), lam_init)
            p2 = _dsa_attention(bq, bk, bv, iq, ik, iw, topk)
            w_out = even_w_out[j].astype(BF16)
        else:
            w, groups, dts = _prep_in_weight(odd_w_in[j], _odd_layout)
            z, xbc, dtr, mq, mk, mv = map(r3, _inproj(h, norm_mix_pre[i], w, cos_t, sin_t, groups, dts, seq))
            p1 = _ssd(xbc, z, dtr, ssm_conv_w[j], ssm_conv_b[j], ssm_dt_bias[j], ssm_a_log[j], ssm_d[j],
                      ssm_norm[j])
            p2 = _moba_attention(mq, mk, mv)
            w_out = odd_w_out[j].astype(BF16)
        k1 = p1.shape[-1]
        h = _outproj(h, p1.reshape(bsz * seq, k1), p2.reshape(bsz * seq, p2.shape[-1]),
                     w_out[:k1], w_out[k1:], norm_mix_post[i])
        h = _ffn(h, norm_ffn_pre[i], ffn_gate[i].astype(BF16), ffn_up[i].astype(BF16),
                 ffn_down[i].astype(BF16), norm_ffn_post[i])
    return h.reshape(bsz, seq, d)
```

```python
import functools
import math

import jax
import jax.numpy as jnp
import numpy as np
from jax import lax
from jax.experimental import pallas as pl
from jax.experimental.pallas import tpu as pltpu

F32 = jnp.float32
BF16 = jnp.bfloat16

D_MODEL = 1024
DIFF_HEADS = 4
DIFF_HEAD_DIM = 64
DSA_HEADS = 4
DSA_HEAD_DIM = 128
IDX_HEADS = 8
IDX_HEAD_DIM = 64
DSA_TOPK_MAX = 256
SSM_HEADS = 16
SSM_HEAD_DIM = 64
SSM_GROUPS = 2
SSM_STATE = 128
SSM_CONV = 4
SSM_INNER = SSM_HEADS * SSM_HEAD_DIM
SSM_CONV_DIM = SSM_INNER + 2 * SSM_GROUPS * SSM_STATE
MOBA_HEADS = 8
MOBA_HEAD_DIM = 64
MOBA_BLOCK = 256
MOBA_TOPK = 3
D_FF = 2816
ROPE_THETA = 10000.0
RMS_EPS = 1e-6

LANES = 128
SUBLANES = 8
VMEM_LIMIT = 56 * 1024 * 1024
NEG = -1e30
INT_MIN = -(2 ** 31)
LOG2E = math.log2(math.e)

TM = 512
TQ = 256

A_QK = 2 * DIFF_HEADS * DIFF_HEAD_DIM
A_V = DIFF_HEADS * 2 * DIFF_HEAD_DIM
B_Q = DSA_HEADS * DSA_HEAD_DIM
I_Q = IDX_HEADS * IDX_HEAD_DIM
M_QKV = MOBA_HEADS * MOBA_HEAD_DIM

PLAIN, ROPE64, ROPE128, TRANS = 0, 1, 2, 3


def _cparams(sem):
    return pltpu.CompilerParams(dimension_semantics=sem, vmem_limit_bytes=VMEM_LIMIT)


def _rms(x, g):
    return x * lax.rsqrt(jnp.mean(x * x, axis=-1, keepdims=True) + RMS_EPS) * g


def _dot(a, b):
    return jnp.dot(a, b, preferred_element_type=F32)


def _dot_nt(a, b):
    return lax.dot_general(a, b, (((1,), (1,)), ((), ())), preferred_element_type=F32)


def _split3(x):
    hi = x.astype(BF16)
    r1 = x - hi.astype(F32)
    mid = r1.astype(BF16)
    lo = (r1 - mid.astype(F32)).astype(BF16)
    return hi, mid, lo


def _dot3(x, m):
    hi, mid, lo = _split3(x)
    return _dot(hi, m) + _dot(mid, m) + _dot(lo, m)


def _inproj_kernel(x_ref, g_ref, w_ref, cos_ref, sin_ref, *out_refs, groups):
    xn = _rms(x_ref[...], g_ref[...]).astype(BF16)
    for (start, width, kind), o_ref in zip(groups, out_refs):
        y = _dot(xn, w_ref[:, start:start + width])
        if kind == PLAIN:
            o_ref[...] = y.astype(o_ref.dtype)
        elif kind == TRANS:
            rows = o_ref.shape[1]
            for blk in range(width // LANES):
                o_ref[blk] = y[:, blk * LANES:(blk + 1) * LANES].T[:rows].astype(o_ref.dtype)
        else:
            cos = cos_ref[kind - 1]
            sin = sin_ref[kind - 1]
            for c in range(0, width, LANES):
                yb = y[:, c:c + LANES]
                o_ref[:, c:c + LANES] = (yb * cos + pltpu.roll(yb, 64, 1) * sin).astype(o_ref.dtype)


def _inproj(h, g, w, cos_t, sin_t, groups, outs, seq):
    t, d = h.shape
    n = w.shape[1]
    tm = TM
    per_seq = seq // tm
    kern = functools.partial(_inproj_kernel, groups=groups)
    out_shape, out_specs = [], []
    for (_, wd, kind), (dt, rows) in zip(groups, outs):
        if kind == TRANS:
            nb = wd // LANES
            out_shape.append(jax.ShapeDtypeStruct((t // tm, nb, rows, tm), dt))
            out_specs.append(pl.BlockSpec((None, nb, rows, tm), lambda i: (i, 0, 0, 0)))
        else:
            out_shape.append(jax.ShapeDtypeStruct((t, wd), dt))
            out_specs.append(pl.BlockSpec((tm, wd), lambda i: (i, 0)))
    return pl.pallas_call(
        kern,
        grid=(t // tm,),
        in_specs=[
            pl.BlockSpec((tm, d), lambda i: (i, 0)),
            pl.BlockSpec((1, d), lambda i: (0, 0)),
            pl.BlockSpec((d, n), lambda i: (0, 0)),
            pl.BlockSpec((2, tm, LANES), lambda i: (0, i % per_seq, 0)),
            pl.BlockSpec((2, tm, LANES), lambda i: (0, i % per_seq, 0)),
        ],
        out_specs=out_specs,
        out_shape=out_shape,
        compiler_params=_cparams(("parallel",)),
        name="inproj",
    )(h, g.reshape(1, d), w, cos_t, sin_t)


def _outproj_kernel(h_ref, a_ref, b_ref, wa_ref, wb_ref, g_ref, o_ref):
    y = _dot(a_ref[...], wa_ref[...]) + _dot(b_ref[...], wb_ref[...])
    o_ref[...] = h_ref[...] + _rms(y, g_ref[...])


def _outproj(h, a, b, wa, wb, g):
    t, d = h.shape
    tm = TM
    ka, kb = a.shape[1], b.shape[1]
    return pl.pallas_call(
        _outproj_kernel,
        grid=(t // tm,),
        in_specs=[
            pl.BlockSpec((tm, d), lambda i: (i, 0)),
            pl.BlockSpec((tm, ka), lambda i: (i, 0)),
            pl.BlockSpec((tm, kb), lambda i: (i, 0)),
            pl.BlockSpec((ka, d), lambda i: (0, 0)),
            pl.BlockSpec((kb, d), lambda i: (0, 0)),
            pl.BlockSpec((1, d), lambda i: (0, 0)),
        ],
        out_specs=pl.BlockSpec((tm, d), lambda i: (i, 0)),
        out_shape=jax.ShapeDtypeStruct((t, d), F32),
        compiler_params=_cparams(("parallel",)),
        name="outproj",
    )(h, a, b, wa, wb, g.reshape(1, d))


def _ffn_kernel(h_ref, gpre_ref, wg_ref, wu_ref, wd_ref, gpost_ref, o_ref, acc_ref, *, fc):
    h = h_ref[...]
    hn = _rms(h, gpre_ref[...]).astype(BF16)
    dff = wg_ref.shape[1]
    for c in range(0, dff, fc):
        gt = _dot(hn, wg_ref[:, c:c + fc])
        up = _dot(hn, wu_ref[:, c:c + fc])
        act = (gt * jax.nn.sigmoid(gt) * up).astype(BF16)
        part = _dot(act, wd_ref[c:c + fc, :])
        if c == 0:
            acc_ref[...] = part
        else:
            acc_ref[...] += part
    o_ref[...] = h + _rms(acc_ref[...], gpost_ref[...])


def _ffn(h, gpre, wg, wu, wd, gpost, fc=256):
    t, d = h.shape
    tm = TM
    dff = wg.shape[1]
    const = dict(pipeline_mode=pl.Buffered(1))
    return pl.pallas_call(
        functools.partial(_ffn_kernel, fc=fc),
        grid=(t // tm,),
        in_specs=[
            pl.BlockSpec((tm, d), lambda i: (i, 0)),
            pl.BlockSpec((1, d), lambda i: (0, 0)),
            pl.BlockSpec((d, dff), lambda i: (0, 0), **const),
            pl.BlockSpec((d, dff), lambda i: (0, 0), **const),
            pl.BlockSpec((dff, d), lambda i: (0, 0), **const),
            pl.BlockSpec((1, d), lambda i: (0, 0)),
        ],
        out_specs=pl.BlockSpec((tm, d), lambda i: (i, 0)),
        out_shape=jax.ShapeDtypeStruct((t, d), F32),
        scratch_shapes=[pltpu.VMEM((tm, d), F32)],
        compiler_params=_cparams(("parallel",)),
        name="ffn",
    )(h, gpre.reshape(1, d), wg, wu, wd, gpost.reshape(1, d))


def _online_t(st, vt, m, l, acc, c):
    m_new = jnp.maximum(m, jnp.max(st, axis=0, keepdims=True) * c)
    alpha = jnp.exp2(m - m_new)
    p = jnp.exp2(st * c - m_new)
    l = alpha * l + jnp.sum(p, axis=0, keepdims=True)
    acc = alpha * acc + _dot(vt, p.astype(BF16))
    return m_new, l, acc


def _pair_queries(q):
    lane = lax.broadcasted_iota(jnp.int32, q.shape, 1)
    is_a = (lane % 64) < 32
    zero = jnp.zeros_like(q)
    return jnp.where(is_a, q, zero), jnp.where(is_a, zero, q)


def _tail_offsets(tk, tq):
    return (lax.broadcasted_iota(jnp.int32, (tk, tq), 0)
            - lax.broadcasted_iota(jnp.int32, (tk, tq), 1))


def _diff_kernel(lam_ref, q_ref, k_ref, vt_ref, g_ref, o_ref, *, tq, tk, out_scale):
    qi = pl.program_id(2)
    ratio = tk // tq
    qs = _pair_queries(q_ref[...])
    c = DIFF_HEAD_DIM ** -0.5 * LOG2E
    diag_keep = _tail_offsets(tk, tq) <= tq * (qi % ratio)

    def step(j, carry, diag):
        kc = k_ref[pl.ds(pl.multiple_of(j * tk, tk), tk), :]
        vt = vt_ref[j]
        out = []
        for mp in range(2):
            st = _dot_nt(kc, qs[mp])
            if diag:
                st = jnp.where(diag_keep, st, NEG)
            out.extend(_online_t(st, vt, *carry[3 * mp:3 * mp + 3], c))
        return tuple(out)

    m0 = jnp.full((1, tq), NEG, F32)
    l0 = jnp.zeros((1, tq), F32)
    a0 = jnp.zeros((LANES, tq), F32)
    carry = (m0, l0, a0, m0, l0, a0)
    nfull = qi // ratio
    carry = lax.fori_loop(0, nfull, lambda j, cr: step(j, cr, False), carry)
    _, l_a, acc_a, _, l_b, acc_b = step(nfull, carry, True)
    o = (acc_a / l_a - lam_ref[0] * (acc_b / l_b)).T
    o_ref[...] = (_rms(o, g_ref[...]) * out_scale).astype(o_ref.dtype)


def _diff_attention(aq, ak, avt, lam, subln_g, lam_init):
    bsz, seq, _ = aq.shape
    tq, tk = TQ, TM
    kern = functools.partial(_diff_kernel, tq=tq, tk=tk, out_scale=1.0 - lam_init)
    return pl.pallas_call(
        kern,
        grid=(bsz, DIFF_HEADS, seq // tq),
        in_specs=[
            pl.BlockSpec(memory_space=pltpu.SMEM),
            pl.BlockSpec((None, tq, LANES), lambda b, h, i: (b, i, h)),
            pl.BlockSpec((None, seq, LANES), lambda b, h, i: (b, 0, h)),
            pl.BlockSpec((None, seq // tk, None, LANES, tk), lambda b, h, i: (b, 0, h, 0, 0)),
            pl.BlockSpec((1, LANES), lambda b, h, i: (0, 0)),
        ],
        out_specs=pl.BlockSpec((None, tq, LANES), lambda b, h, i: (b, i, h)),
        out_shape=jax.ShapeDtypeStruct((bsz, seq, A_V), BF16),
        compiler_params=_cparams(("parallel", "parallel", "arbitrary")),
        name="diff_attn",
    )(lam.reshape(1), aq, ak, avt, subln_g.reshape(1, LANES))


def _dsa_kernel(q_ref, k_ref, vt_ref, iq_ref, ik_ref, iwt_ref, o_ref, key_sc, m_sc, l_sc, acc_sc,
                *, tq, tk, topk):
    qi = pl.program_id(1)
    ratio = tk // tq
    nfull = qi // ratio
    nch = nfull + 1
    diag_keep = _tail_offsets(tk, tq) <= tq * (qi % ratio)
    key_pos = lax.broadcasted_iota(jnp.int32, (tk, tq), 0)
    min_key = jnp.int32(INT_MIN)

    wt = iwt_ref[...] * (IDX_HEADS ** -0.5 * IDX_HEAD_DIM ** -0.5)
    qms = []
    for blk in range(IDX_HEADS // 2):
        qms.extend(_pair_queries(iq_ref[:, blk * LANES:(blk + 1) * LANES]))

    def score_chunk(j, diag):
        kc = ik_ref[pl.ds(pl.multiple_of(j * tk, tk), tk), :]
        sc = jnp.zeros((tk, tq), F32)
        for hh in range(IDX_HEADS):
            sc = sc + wt[hh:hh + 1, :] * jnp.maximum(_dot_nt(kc, qms[hh]), 0.0)
        bits = pltpu.bitcast(sc, jnp.int32)
        key = bits ^ ((bits >> 31) & jnp.int32(0x7FFFFFFF))
        if diag:
            key = jnp.where(diag_keep, key, min_key)
        key_sc[j] = key

    def score_body(j, cr):
        score_chunk(j, False)
        return cr

    lax.fori_loop(0, nfull, score_body, 0)
    score_chunk(nfull, True)

    def count(pred_fn):
        def body(j, cnt):
            return cnt + jnp.sum(jnp.where(pred_fn(key_sc[j], j), 1.0, 0.0), axis=0, keepdims=True)
        return lax.fori_loop(0, nch, body, jnp.zeros((1, tq), F32))

    kf = float(topk)
    thr = jnp.where(count(lambda key, j: key >= 0) >= kf, jnp.int32(0), min_key)

    def bit_body(bi, thr):
        cand = thr | (jnp.int32(1) << (30 - bi))
        return jnp.where(count(lambda key, j: key >= cand) >= kf, cand, thr)

    thr = lax.fori_loop(0, 31, bit_body, thr)
    thr = jnp.maximum(thr, min_key + 1)

    cnt_ge = count(lambda key, j: key >= thr)

    @pl.when(jnp.max(cnt_ge) > kf)
    def _():
        need = kf - count(lambda key, j: key > thr)

        def pos_body(bi, cut):
            cand = cut | (jnp.int32(1) << (30 - bi))
            cnt = count(lambda key, j: (key == thr) & (key_pos + j * tk < cand))
            return jnp.where(cnt < need, cand, cut)

        cut = lax.fori_loop(0, 31, pos_body, jnp.zeros((1, tq), jnp.int32))

        def demote(j, cr):
            key = key_sc[j]
            key_sc[j] = jnp.where((key == thr) & (key_pos + j * tk > cut), thr - 1, key)
            return cr

        lax.fori_loop(0, nch, demote, 0)

    c = DSA_HEAD_DIM ** -0.5 * LOG2E
    m_sc[...] = jnp.full(m_sc.shape, NEG, F32)
    l_sc[...] = jnp.zeros(l_sc.shape, F32)
    acc_sc[...] = jnp.zeros(acc_sc.shape, F32)

    def att_body(j, cr):
        kc = k_ref[pl.ds(pl.multiple_of(j * tk, tk), tk), :]
        vt = vt_ref[j]
        sel = key_sc[j] >= thr
        for hh in range(DSA_HEADS):
            st = jnp.where(sel, _dot_nt(kc, q_ref[:, hh * LANES:(hh + 1) * LANES]), NEG)
            m, l, acc = _online_t(st, vt, m_sc[hh], l_sc[hh], acc_sc[hh], c)
            m_sc[hh] = m
            l_sc[hh] = l
            acc_sc[hh] = acc
        return cr

    lax.fori_loop(0, nch, att_body, 0)
    for hh in range(DSA_HEADS):
        o_ref[:, hh * LANES:(hh + 1) * LANES] = (acc_sc[hh] / l_sc[hh]).T.astype(o_ref.dtype)


def _dsa_attention(bq, bk, bvt, iq, ik, iwt, topk):
    bsz, seq, _ = bq.shape
    tq, tk = TQ, TM
    ratio = tk // tq
    kern = functools.partial(_dsa_kernel, tq=tq, tk=tk, topk=topk)
    full = lambda b, i: (b, 0, 0)
    tile = lambda b, i: (b, i, 0)
    return pl.pallas_call(
        kern,
        grid=(bsz, seq // tq),
        in_specs=[
            pl.BlockSpec((None, tq, B_Q), tile),
            pl.BlockSpec((None, seq, LANES), full),
            pl.BlockSpec((None, seq // tk, None, LANES, tk), lambda b, i: (b, 0, 0, 0, 0)),
            pl.BlockSpec((None, tq, I_Q), tile),
            pl.BlockSpec((None, seq, LANES), full),
            pl.BlockSpec((None, None, None, SUBLANES, tq), lambda b, i: (b, i // ratio, 0, 0, i % ratio)),
        ],
        out_specs=pl.BlockSpec((None, tq, B_Q), tile),
        out_shape=jax.ShapeDtypeStruct((bsz, seq, B_Q), BF16),
        scratch_shapes=[
            pltpu.VMEM((seq // tk, tk, tq), jnp.int32),
            pltpu.VMEM((DSA_HEADS, 1, tq), F32),
            pltpu.VMEM((DSA_HEADS, 1, tq), F32),
            pltpu.VMEM((DSA_HEADS, LANES, tq), F32),
        ],
        compiler_params=_cparams(("parallel", "arbitrary")),
        name="dsa",
    )(bq, bk, bvt, iq, ik, iwt)


def _ssd_kernel(xbc_ref, z_ref, dt_ref, cw_ref, cb_ref, dtb_ref, alog_ref, dskip_ref, ng_ref,
                e_pair_ref, e_full_ref, o_ref, xext_sc, state_sc, *, lc):
    ci = pl.program_id(1)
    npair = SSM_HEADS // 2
    gw = SSM_INNER // SSM_GROUPS

    @pl.when(ci == 0)
    def _():
        xext_sc[0:8, :] = jnp.zeros((8, SSM_CONV_DIM), F32)
        state_sc[...] = jnp.zeros(state_sc.shape, F32)

    xext_sc[8:8 + lc, :] = xbc_ref[...].astype(F32)
    conv = cb_ref[...] + cw_ref[SSM_CONV - 1:SSM_CONV, :] * xext_sc[8:8 + lc, :]
    for sh in range(1, SSM_CONV):
        conv = conv + cw_ref[SSM_CONV - 1 - sh:SSM_CONV - sh, :] * xext_sc[8 - sh:8 - sh + lc, :]
    xext_sc[0:8, :] = xext_sc[lc:lc + 8, :]
    xbc = conv * jax.nn.sigmoid(conv)
    xs = xbc[:, :SSM_INNER]
    bm = [xbc[:, SSM_INNER + g * SSM_STATE:SSM_INNER + (g + 1) * SSM_STATE] for g in range(SSM_GROUPS)]
    cm = [xbc[:, SSM_INNER + (SSM_GROUPS + g) * SSM_STATE:SSM_INNER + (SSM_GROUPS + g + 1) * SSM_STATE]
          for g in range(SSM_GROUPS)]

    dtr = dt_ref[...] + dtb_ref[...]
    dt = jnp.maximum(dtr, 0.0) + jnp.log1p(jnp.exp(-jnp.abs(dtr)))
    da = dt * (-jnp.exp(alog_ref[...]))
    r_i = lax.broadcasted_iota(jnp.int32, (lc, lc), 0)
    c_i = lax.broadcasted_iota(jnp.int32, (lc, lc), 1)
    tri = c_i <= r_i
    tri_bf = jnp.where(tri, 1.0, 0.0).astype(BF16)
    hi, mid, lo = _split3(da)
    acum = _dot(tri_bf, hi) + _dot(tri_bf, mid) + _dot(tri_bf, lo)
    acum_t = acum.T
    tail_t = jnp.exp(acum_t[:, lc - 1:lc] - acum_t)

    e_pair = e_pair_ref[...]
    e_full = e_full_ref[...]
    dt_x = _dot3(dt, e_pair)
    acum_x = _dot3(acum, e_pair)
    alast_x = acum_x[lc - 1:lc, :]
    acum_cb = _dot3(acum, e_full)

    xdt = (xs * dt_x).astype(BF16)
    lane = lax.broadcasted_iota(jnp.int32, (lc, LANES), 1)
    first = lane < SSM_HEAD_DIM
    lane_n = lax.broadcasted_iota(jnp.int32, (SSM_STATE, LANES), 1)
    first_n = lane_n < SSM_HEAD_DIM

    ys = []
    for g in range(SSM_GROUPS):
        cg = cm[g].astype(BF16)
        bg = bm[g].astype(BF16)
        cb = _dot_nt(cg, bg)
        bm_t = bm[g].T
        for pp in range(g * npair // SSM_GROUPS, (g + 1) * npair // SSM_GROUPS):
            xp = xdt[:, pp * LANES:(pp + 1) * LANES]
            yh, sh_new = [], []
            for sub in range(2):
                hd = 2 * pp + sub
                seg = acum_cb[:, hd * LANES:(hd + 1) * LANES] - acum_t[hd:hd + 1, :]
                decay = jnp.exp(jnp.where(tri, seg, -jnp.inf))
                yh.append(_dot((cb * decay).astype(BF16), xp))
                sh_new.append(_dot((bm_t * tail_t[hd:hd + 1, :]).astype(BF16), xp))
            st = state_sc[pp]
            y_in = _dot(cg, st.astype(BF16)) * jnp.exp(acum_x[:, pp * LANES:(pp + 1) * LANES])
            ys.append(jnp.where(first, yh[0], yh[1]) + y_in)
            state_sc[pp] = (st * jnp.exp(alast_x[:, pp * LANES:(pp + 1) * LANES])
                            + jnp.where(first_n, sh_new[0], sh_new[1]))
    y = jnp.concatenate(ys, axis=-1) + dskip_ref[...] * xs
    zf = z_ref[...].astype(F32)
    y = y * (zf * jax.nn.sigmoid(zf))
    for g in range(SSM_GROUPS):
        sl = slice(g * gw, (g + 1) * gw)
        o_ref[:, sl] = _rms(y[:, sl], ng_ref[:, sl]).astype(o_ref.dtype)


def _ssd(xbc, z, dt_raw, conv_w, conv_b, dt_bias, a_log, d_skip, norm_g, lc=128):
    bsz, seq, _ = xbc.shape
    pad = lambda v: jnp.pad(v.astype(F32), (0, LANES - v.shape[0])).reshape(1, LANES)
    heads = np.arange(LANES)[:, None]
    e_pair = jnp.asarray(heads == (np.arange(SSM_INNER)[None, :] // SSM_HEAD_DIM), BF16)
    e_full = jnp.asarray(heads == (np.arange(SSM_HEADS * LANES)[None, :] // LANES), BF16)
    dskip_x = jnp.repeat(d_skip.astype(F32), SSM_HEAD_DIM).reshape(1, SSM_INNER)
    tile = lambda b, c: (b, c, 0)
    const = lambda b, c: (0, 0)
    return pl.pallas_call(
        functools.partial(_ssd_kernel, lc=lc),
        grid=(bsz, seq // lc),
        in_specs=[
            pl.BlockSpec((None, lc, SSM_CONV_DIM), tile),
            pl.BlockSpec((None, lc, SSM_INNER), tile),
            pl.BlockSpec((None, lc, LANES), tile),
            pl.BlockSpec((SSM_CONV, SSM_CONV_DIM), const),
            pl.BlockSpec((1, SSM_CONV_DIM), const),
            pl.BlockSpec((1, LANES), const),
            pl.BlockSpec((1, LANES), const),
            pl.BlockSpec((1, SSM_INNER), const),
            pl.BlockSpec((1, SSM_INNER), const),
            pl.BlockSpec((LANES, SSM_INNER), const),
            pl.BlockSpec((LANES, SSM_HEADS * LANES), const),
        ],
        out_specs=pl.BlockSpec((None, lc, SSM_INNER), tile),
        out_shape=jax.ShapeDtypeStruct((bsz, seq, SSM_INNER), BF16),
        scratch_shapes=[
            pltpu.VMEM((lc + 8, SSM_CONV_DIM), F32),
            pltpu.VMEM((SSM_HEADS // 2, SSM_STATE, LANES), F32),
        ],
        compiler_params=_cparams(("parallel", "arbitrary")),
        name="ssd",
    )(xbc, z, dt_raw, conv_w.astype(F32), conv_b.astype(F32).reshape(1, SSM_CONV_DIM), pad(dt_bias),
      pad(a_log), dskip_x, norm_g.astype(F32).reshape(1, SSM_INNER), e_pair, e_full)


def _moba_kernel(q_ref, k_ref, vt_ref, o_ref, kmean_sc, sel_sc, *, nblk, tk):
    qi = pl.program_id(2)
    tq = MOBA_BLOCK
    ratio = tk // tq
    hd = MOBA_HEAD_DIM
    c = hd ** -0.5 * LOG2E
    nrow = kmean_sc.shape[0]

    @pl.when(qi == 0)
    def _():
        kmean_sc[...] = jnp.zeros(kmean_sc.shape, F32)
        for j in range(nblk):
            kb = k_ref[j * tq:(j + 1) * tq, :].astype(F32)
            kmean_sc[j:j + 1, :] = jnp.sum(kb, axis=0, keepdims=True) * (1.0 / tq)

    qs = _pair_queries(q_ref[...])
    kmean = kmean_sc[...].astype(BF16)
    blk_id = lax.broadcasted_iota(jnp.int32, (nrow, tq), 0)
    blk_f = blk_id.astype(F32)
    causal = _tail_offsets(tq, tq) <= 0
    nfull = qi // ratio

    outs = []
    for sub in range(2):
        qm = qs[sub]
        gate = jnp.where(blk_id < qi, _dot_nt(kmean, qm), -jnp.inf)
        sel = jnp.zeros((nrow, tq), F32)
        for _ in range(MOBA_TOPK):
            mx = jnp.max(gate, axis=0, keepdims=True)
            first = jnp.min(jnp.where(gate == mx, blk_f, float(nrow)), axis=0, keepdims=True)
            hit = (blk_f == first) & (mx > -jnp.inf)
            sel = jnp.where(hit, 1.0, sel)
            gate = jnp.where(hit, -jnp.inf, gate)
        sel_sc[sub] = sel
        vt_rows = slice(sub * hd, (sub + 1) * hd)

        def step(j, carry, diag):
            kc = k_ref[pl.ds(pl.multiple_of(j * tk, tk), tk), :]
            vt = vt_ref[j][vt_rows, :]
            st = _dot_nt(kc, qm)
            parts = []
            for hb in range(ratio):
                blk = j * ratio + hb
                part = st[hb * tq:(hb + 1) * tq, :]
                keep = sel_sc[sub, pl.ds(blk, 1), :] > 0.5
                if diag:
                    keep = (keep & (blk < qi)) | (causal & (blk == qi))
                parts.append(jnp.where(keep, part, NEG))
            return _online_t(jnp.concatenate(parts, axis=0), vt, *carry, c)

        carry = (jnp.full((1, tq), NEG, F32), jnp.zeros((1, tq), F32), jnp.zeros((hd, tq), F32))
        carry = lax.fori_loop(0, nfull, lambda j, cr: step(j, cr, False), carry)
        _, l, acc = step(nfull, carry, True)
        outs.append(acc / l)
    o_ref[...] = jnp.concatenate(outs, axis=0).T.astype(o_ref.dtype)


def _moba_attention(mq, mk, mvt):
    bsz, seq, _ = mq.shape
    nblk = seq // MOBA_BLOCK
    tq, tk = MOBA_BLOCK, TM
    nrow = -(-nblk // 16) * 16
    return pl.pallas_call(
        functools.partial(_moba_kernel, nblk=nblk, tk=tk),
        grid=(bsz, MOBA_HEADS // 2, nblk),
        in_specs=[
            pl.BlockSpec((None, tq, LANES), lambda b, p, i: (b, i, p)),
            pl.BlockSpec((None, seq, LANES), lambda b, p, i: (b, 0, p)),
            pl.BlockSpec((None, seq // tk, None, LANES, tk), lambda b, p, i: (b, 0, p, 0, 0)),
        ],
        out_specs=pl.BlockSpec((None, tq, LANES), lambda b, p, i: (b, i, p)),
        out_shape=jax.ShapeDtypeStruct((bsz, seq, M_QKV), BF16),
        scratch_shapes=[pltpu.VMEM((nrow, LANES), F32), pltpu.VMEM((2, nrow, tq), F32)],
        compiler_params=_cparams(("parallel", "parallel", "arbitrary")),
        name="moba",
    )(mq, mk, mvt)


_PERM64 = np.concatenate([np.arange(0, 32), np.arange(64, 96), np.arange(32, 64), np.arange(96, 128)])
_DUP64 = np.concatenate([np.arange(0, 32), np.arange(0, 32), np.arange(32, 64), np.arange(32, 64)])


def _perm_blocks(start, width):
    return np.concatenate([start + c + _PERM64 for c in range(0, width, LANES)])


def _rope_tables(seq):
    def tab(dim):
        inv = ROPE_THETA ** (-jnp.arange(0, dim, 2, dtype=F32) / dim)
        ang = jnp.arange(seq, dtype=F32)[:, None] * inv[None, :]
        return jnp.cos(ang), jnp.sin(ang)
    c64, s64 = tab(64)
    c128, s128 = tab(128)
    cos_t = jnp.stack([jnp.tile(c64, (1, 4)), jnp.tile(c128, (1, 2))])
    sin_t = jnp.stack([jnp.concatenate([-s64, -s64, s64, s64], axis=1),
                       jnp.concatenate([-s128, s128], axis=1)])
    return cos_t, sin_t


def _even_layout(pad_col):
    o = np.cumsum([0, A_QK, A_QK, A_V, B_Q, DSA_HEAD_DIM, DSA_HEAD_DIM, I_Q, IDX_HEAD_DIM, IDX_HEADS])
    pad = np.full(LANES - IDX_HEADS, pad_col)
    return [
        (_perm_blocks(o[0], A_QK), ROPE64, BF16, None),
        (_perm_blocks(o[1], A_QK), ROPE64, BF16, None),
        (np.arange(o[2], o[3]), TRANS, BF16, LANES),
        (np.arange(o[3], o[4]), ROPE128, BF16, None),
        (np.arange(o[4], o[5]), ROPE128, BF16, None),
        (np.arange(o[5], o[6]), TRANS, BF16, LANES),
        (_perm_blocks(o[6], I_Q), ROPE64, BF16, None),
        (o[7] + _DUP64, ROPE64, BF16, None),
        (np.concatenate([np.arange(o[8], o[9]), pad]), TRANS, F32, IDX_HEADS),
    ]


def _odd_layout(pad_col):
    o = np.cumsum([0, SSM_INNER, SSM_CONV_DIM, SSM_HEADS, M_QKV, M_QKV, M_QKV])
    pad = np.full(LANES - SSM_HEADS, pad_col)
    return [
        (np.arange(o[0], o[1]), PLAIN, BF16, None),
        (np.arange(o[1], o[2]), PLAIN, BF16, None),
        (np.concatenate([np.arange(o[2], o[3]), pad]), PLAIN, F32, None),
        (_perm_blocks(o[3], M_QKV), ROPE64, BF16, None),
        (_perm_blocks(o[4], M_QKV), ROPE64, BF16, None),
        (np.arange(o[5], o[6]), TRANS, BF16, LANES),
    ]


def _prep_in_weight(w, layout_fn):
    layout = layout_fn(w.shape[1])
    cols = np.concatenate([c for c, _, _, _ in layout])
    wl = jnp.concatenate([w, jnp.zeros((w.shape[0], 1), w.dtype)], axis=1)[:, cols].astype(BF16)
    widths = [len(c) for c, _, _, _ in layout]
    starts = np.concatenate([[0], np.cumsum(widths)[:-1]])
    groups = tuple((int(s), int(wd), int(k)) for s, wd, (_, k, _, _) in zip(starts, widths, layout))
    return wl, groups, [(dt, rows) for _, _, dt, rows in layout]


def kernel(x, norm_mix_pre, norm_mix_post, norm_ffn_pre, norm_ffn_post, ffn_gate, ffn_up, ffn_down,
           even_w_in, even_w_out, diff_lambda, diff_subln, odd_w_in, odd_w_out, ssm_conv_w,
           ssm_conv_b, ssm_dt_bias, ssm_a_log, ssm_d, ssm_norm):
    bsz, seq, d = x.shape
    depth = norm_mix_pre.shape[0]
    cos_t, sin_t = _rope_tables(seq)
    topk = min(DSA_TOPK_MAX, seq // 4)
    h = x.reshape(bsz * seq, d)

    def per_batch(t):
        if t.ndim == 2:
            return t.reshape(bsz, seq, t.shape[-1])
        return t.reshape(bsz, seq // TM, *t.shape[1:])

    for i in range(depth):
        j = i // 2
        if i % 2 == 0:
            w, groups, outs = _prep_in_weight(even_w_in[j], _even_layout)
            aq, ak, avt, bq, bk, bvt, iq, ik, iwt = map(
                per_batch, _inproj(h, norm_mix_pre[i], w, cos_t, sin_t, groups, outs, seq))
            lam_init = 0.8 - 0.6 * math.exp(-0.3 * i)
            lf = diff_lambda[j].astype(F32)
            lam = jnp.exp(jnp.sum(lf[0] * lf[1])) - jnp.exp(jnp.sum(lf[2] * lf[3])) + lam_init
            p1 = _diff_attention(aq, ak, avt, lam, diff_subln[j].astype(F32), lam_init)
            p2 = _dsa_attention(bq, bk, bvt, iq, ik, iwt, topk)
            w_out = even_w_out[j].astype(BF16)
        else:
            w, groups, outs = _prep_in_weight(odd_w_in[j], _odd_layout)
            z, xbc, dtr, mq, mk, mvt = map(
                per_batch, _inproj(h, norm_mix_pre[i], w, cos_t, sin_t, groups, outs, seq))
            p1 = _ssd(xbc, z, dtr, ssm_conv_w[j], ssm_conv_b[j], ssm_dt_bias[j], ssm_a_log[j], ssm_d[j],
                      ssm_norm[j])
            p2 = _moba_attention(mq, mk, mvt)
            w_out = odd_w_out[j].astype(BF16)
        k1 = p1.shape[-1]
        h = _outproj(h, p1.reshape(bsz * seq, k1), p2.reshape(bsz * seq, p2.shape[-1]),
                     w_out[:k1], w_out[k1:], norm_mix_post[i])
        h = _ffn(h, norm_ffn_pre[i], ffn_gate[i].astype(BF16), ffn_up[i].astype(BF16),
                 ffn_down[i].astype(BF16), norm_ffn_post[i])
    return h.reshape(bsz, seq, d)
```

```python
import functools
import math

import jax
import jax.numpy as jnp
import numpy as np
from jax import lax
from jax.experimental import pallas as pl
from jax.experimental.pallas import tpu as pltpu

F32 = jnp.float32
BF16 = jnp.bfloat16

D_MODEL = 1024
DIFF_HEADS = 4
DIFF_HEAD_DIM = 64
DSA_HEADS = 4
DSA_HEAD_DIM = 128
IDX_HEADS = 8
IDX_HEAD_DIM = 64
DSA_TOPK_MAX = 256
SSM_HEADS = 16
SSM_HEAD_DIM = 64
SSM_GROUPS = 2
SSM_STATE = 128
SSM_CONV = 4
SSM_INNER = SSM_HEADS * SSM_HEAD_DIM
SSM_CONV_DIM = SSM_INNER + 2 * SSM_GROUPS * SSM_STATE
MOBA_HEADS = 8
MOBA_HEAD_DIM = 64
MOBA_BLOCK = 256
MOBA_TOPK = 3
D_FF = 2816
ROPE_THETA = 10000.0
RMS_EPS = 1e-6

LANES = 128
SUBLANES = 8
VMEM_LIMIT = 56 * 1024 * 1024
NEG = -1e30
INT_MIN = -(2 ** 31)
LOG2E = math.log2(math.e)

TM = 512
TQ = 256
ROW_BLOCK = 64

A_QK = 2 * DIFF_HEADS * DIFF_HEAD_DIM
A_V = DIFF_HEADS * 2 * DIFF_HEAD_DIM
B_Q = DSA_HEADS * DSA_HEAD_DIM
I_Q = IDX_HEADS * IDX_HEAD_DIM
M_QKV = MOBA_HEADS * MOBA_HEAD_DIM

PLAIN, ROPE64, ROPE128, TRANS = 0, 1, 2, 3


def _cparams(sem):
    return pltpu.CompilerParams(dimension_semantics=sem, vmem_limit_bytes=VMEM_LIMIT)


def _rms(x, g):
    return x * lax.rsqrt(jnp.mean(x * x, axis=-1, keepdims=True) + RMS_EPS) * g


def _dot(a, b):
    return jnp.dot(a, b, preferred_element_type=F32)


def _dot_nt(a, b):
    return lax.dot_general(a, b, (((1,), (1,)), ((), ())), preferred_element_type=F32)


def _split3(x):
    hi = x.astype(BF16)
    r1 = x - hi.astype(F32)
    mid = r1.astype(BF16)
    lo = (r1 - mid.astype(F32)).astype(BF16)
    return hi, mid, lo


def _dot3(x, m):
    hi, mid, lo = _split3(x)
    return _dot(hi, m) + _dot(mid, m) + _dot(lo, m)


def _inproj_kernel(x_ref, g_ref, w_ref, cos_ref, sin_ref, *out_refs, groups):
    xn = _rms(x_ref[...], g_ref[...]).astype(BF16)
    for (start, width, kind), o_ref in zip(groups, out_refs):
        y = _dot(xn, w_ref[:, start:start + width])
        if kind == PLAIN:
            o_ref[...] = y.astype(o_ref.dtype)
        elif kind == TRANS:
            rows = o_ref.shape[1]
            for blk in range(width // LANES):
                o_ref[blk] = y[:, blk * LANES:(blk + 1) * LANES].T[:rows].astype(o_ref.dtype)
        else:
            cos = cos_ref[kind - 1]
            sin = sin_ref[kind - 1]
            for c in range(0, width, LANES):
                yb = y[:, c:c + LANES]
                o_ref[:, c:c + LANES] = (yb * cos + pltpu.roll(yb, 64, 1) * sin).astype(o_ref.dtype)


def _inproj(h, g, w, cos_t, sin_t, groups, outs, seq):
    t, d = h.shape
    n = w.shape[1]
    tm = TM
    per_seq = seq // tm
    kern = functools.partial(_inproj_kernel, groups=groups)
    out_shape, out_specs = [], []
    for (_, wd, kind), (dt, rows) in zip(groups, outs):
        if kind == TRANS:
            nb = wd // LANES
            out_shape.append(jax.ShapeDtypeStruct((t // tm, nb, rows, tm), dt))
            out_specs.append(pl.BlockSpec((None, nb, rows, tm), lambda i: (i, 0, 0, 0)))
        else:
            out_shape.append(jax.ShapeDtypeStruct((t, wd), dt))
            out_specs.append(pl.BlockSpec((tm, wd), lambda i: (i, 0)))
    return pl.pallas_call(
        kern,
        grid=(t // tm,),
        in_specs=[
            pl.BlockSpec((tm, d), lambda i: (i, 0)),
            pl.BlockSpec((1, d), lambda i: (0, 0)),
            pl.BlockSpec((d, n), lambda i: (0, 0)),
            pl.BlockSpec((2, tm, LANES), lambda i: (0, i % per_seq, 0)),
            pl.BlockSpec((2, tm, LANES), lambda i: (0, i % per_seq, 0)),
        ],
        out_specs=out_specs,
        out_shape=out_shape,
        compiler_params=_cparams(("parallel",)),
        name="inproj",
    )(h, g.reshape(1, d), w, cos_t, sin_t)


def _outproj_kernel(h_ref, a_ref, b_ref, wa_ref, wb_ref, g_ref, o_ref):
    y = _dot(a_ref[...], wa_ref[...]) + _dot(b_ref[...], wb_ref[...])
    o_ref[...] = h_ref[...] + _rms(y, g_ref[...])


def _outproj(h, a, b, wa, wb, g):
    t, d = h.shape
    tm = TM
    ka, kb = a.shape[1], b.shape[1]
    return pl.pallas_call(
        _outproj_kernel,
        grid=(t // tm,),
        in_specs=[
            pl.BlockSpec((tm, d), lambda i: (i, 0)),
            pl.BlockSpec((tm, ka), lambda i: (i, 0)),
            pl.BlockSpec((tm, kb), lambda i: (i, 0)),
            pl.BlockSpec((ka, d), lambda i: (0, 0)),
            pl.BlockSpec((kb, d), lambda i: (0, 0)),
            pl.BlockSpec((1, d), lambda i: (0, 0)),
        ],
        out_specs=pl.BlockSpec((tm, d), lambda i: (i, 0)),
        out_shape=jax.ShapeDtypeStruct((t, d), F32),
        compiler_params=_cparams(("parallel",)),
        name="outproj",
    )(h, a, b, wa, wb, g.reshape(1, d))


def _ffn_kernel(h_ref, gpre_ref, wg_ref, wu_ref, wd_ref, gpost_ref, o_ref, acc_ref, *, fc):
    h = h_ref[...]
    hn = _rms(h, gpre_ref[...]).astype(BF16)
    dff = wg_ref.shape[1]
    for c in range(0, dff, fc):
        gt = _dot(hn, wg_ref[:, c:c + fc])
        up = _dot(hn, wu_ref[:, c:c + fc])
        act = (gt * jax.nn.sigmoid(gt) * up).astype(BF16)
        part = _dot(act, wd_ref[c:c + fc, :])
        if c == 0:
            acc_ref[...] = part
        else:
            acc_ref[...] += part
    o_ref[...] = h + _rms(acc_ref[...], gpost_ref[...])


def _ffn(h, gpre, wg, wu, wd, gpost, fc=256):
    t, d = h.shape
    tm = TM
    dff = wg.shape[1]
    const = dict(pipeline_mode=pl.Buffered(1))
    return pl.pallas_call(
        functools.partial(_ffn_kernel, fc=fc),
        grid=(t // tm,),
        in_specs=[
            pl.BlockSpec((tm, d), lambda i: (i, 0)),
            pl.BlockSpec((1, d), lambda i: (0, 0)),
            pl.BlockSpec((d, dff), lambda i: (0, 0), **const),
            pl.BlockSpec((d, dff), lambda i: (0, 0), **const),
            pl.BlockSpec((dff, d), lambda i: (0, 0), **const),
            pl.BlockSpec((1, d), lambda i: (0, 0)),
        ],
        out_specs=pl.BlockSpec((tm, d), lambda i: (i, 0)),
        out_shape=jax.ShapeDtypeStruct((t, d), F32),
        scratch_shapes=[pltpu.VMEM((tm, d), F32)],
        compiler_params=_cparams(("parallel",)),
        name="ffn",
    )(h, gpre.reshape(1, d), wg, wu, wd, gpost.reshape(1, d))


def _online_t(st, vt, m, l, acc, c):
    m_new = jnp.maximum(m, jnp.max(st, axis=0, keepdims=True) * c)
    alpha = jnp.exp2(m - m_new)
    p = jnp.concatenate([jnp.exp2(st[r:r + ROW_BLOCK] * c - m_new).astype(BF16)
                         for r in range(0, st.shape[0], ROW_BLOCK)], axis=0)
    ones = jnp.ones((SUBLANES, p.shape[0]), BF16)
    l = alpha * l + _dot(ones, p)[:1]
    acc = alpha * acc + _dot(vt, p)
    return m_new, l, acc


def _pair_queries(q):
    lane = lax.broadcasted_iota(jnp.int32, q.shape, 1)
    is_a = (lane % 64) < 32
    zero = jnp.zeros_like(q)
    return jnp.where(is_a, q, zero), jnp.where(is_a, zero, q)


def _flash_chunks(n_items, logits_fn, values_fn, scratch, c):
    st_a, st_b, m_sc, l_sc, acc_sc = scratch
    m_sc[...] = jnp.full(m_sc.shape, NEG, F32)
    l_sc[...] = jnp.zeros(l_sc.shape, F32)
    acc_sc[...] = jnp.zeros(acc_sc.shape, F32)

    def fill(buf, item, first=False):
        for s, st in enumerate(logits_fn(item, first)):
            buf[s] = st

    def drain(buf, item):
        for s in range(buf.shape[0]):
            m, l, acc = _online_t(buf[s], values_fn(item, s), m_sc[s], l_sc[s], acc_sc[s], c)
            m_sc[s] = m
            l_sc[s] = l
            acc_sc[s] = acc

    fill(st_a, 0, first=True)

    def body(t, carry):
        fill(st_b, 2 * t + 1)
        drain(st_a, 2 * t)
        fill(st_a, 2 * t + 2)
        drain(st_b, 2 * t + 1)
        return carry

    lax.fori_loop(0, n_items // 2, body, 0)

    @pl.when(n_items % 2 == 1)
    def _():
        drain(st_a, n_items - 1)


def _flash_scratch(streams, tk, tq, dv):
    return [pltpu.VMEM((streams, tk, tq), F32), pltpu.VMEM((streams, tk, tq), F32),
            pltpu.VMEM((streams, 1, tq), F32), pltpu.VMEM((streams, 1, tq), F32),
            pltpu.VMEM((streams, dv, tq), F32)]


def _tail_offsets(tk, tq):
    return (lax.broadcasted_iota(jnp.int32, (tk, tq), 0)
            - lax.broadcasted_iota(jnp.int32, (tk, tq), 1))


def _diff_kernel(lam_ref, q_ref, k_ref, vt_ref, g_ref, o_ref, *scratch, tq, tk, out_scale):
    qi = pl.program_id(2)
    ratio = tk // tq
    nfull = qi // ratio
    qs = _pair_queries(q_ref[...])
    c = DIFF_HEAD_DIM ** -0.5 * LOG2E

    def chunk_of(item):
        return jnp.where(item == 0, nfull, jnp.minimum(item - 1, nfull))

    def logits(item, first):
        kc = k_ref[pl.ds(pl.multiple_of(chunk_of(item) * tk, tk), tk), :]
        sts = tuple(_dot_nt(kc, qm) for qm in qs)
        if first:
            diag_keep = _tail_offsets(tk, tq) <= tq * (qi % ratio)
            sts = tuple(jnp.where(diag_keep, st, NEG) for st in sts)
        return sts

    _flash_chunks(nfull + 1, logits, lambda item, s: vt_ref[chunk_of(item)], scratch, c)
    _, _, _, l_sc, acc_sc = scratch
    o = (acc_sc[0] / l_sc[0] - lam_ref[0] * (acc_sc[1] / l_sc[1])).T
    o_ref[...] = (_rms(o, g_ref[...]) * out_scale).astype(o_ref.dtype)


def _diff_attention(aq, ak, avt, lam, subln_g, lam_init):
    bsz, seq, _ = aq.shape
    tq, tk = TQ, TM
    kern = functools.partial(_diff_kernel, tq=tq, tk=tk, out_scale=1.0 - lam_init)
    return pl.pallas_call(
        kern,
        grid=(bsz, DIFF_HEADS, seq // tq),
        in_specs=[
            pl.BlockSpec(memory_space=pltpu.SMEM),
            pl.BlockSpec((None, tq, LANES), lambda b, h, i: (b, i, h)),
            pl.BlockSpec((None, seq, LANES), lambda b, h, i: (b, 0, h)),
            pl.BlockSpec((None, seq // tk, None, LANES, tk), lambda b, h, i: (b, 0, h, 0, 0)),
            pl.BlockSpec((1, LANES), lambda b, h, i: (0, 0)),
        ],
        out_specs=pl.BlockSpec((None, tq, LANES), lambda b, h, i: (b, i, h)),
        out_shape=jax.ShapeDtypeStruct((bsz, seq, A_V), BF16),
        scratch_shapes=_flash_scratch(2, tk, tq, LANES),
        compiler_params=_cparams(("parallel", "parallel", "arbitrary")),
        name="diff_attn",
    )(lam.reshape(1), aq, ak, avt, subln_g.reshape(1, LANES))


def _dsa_kernel(q_ref, k_ref, vt_ref, iq_ref, ik_ref, iwt_ref, o_ref, key_sc, *scratch, tq, tk, topk):
    qi = pl.program_id(1)
    ratio = tk // tq
    nfull = qi // ratio
    nch = nfull + 1
    diag_keep = _tail_offsets(tk, tq) <= tq * (qi % ratio)
    key_pos = lax.broadcasted_iota(jnp.int32, (tk, tq), 0)
    min_key = jnp.int32(INT_MIN)

    wt = iwt_ref[...] * (IDX_HEADS ** -0.5 * IDX_HEAD_DIM ** -0.5)
    qms = []
    for blk in range(IDX_HEADS // 2):
        qms.extend(_pair_queries(iq_ref[:, blk * LANES:(blk + 1) * LANES]))

    def score_chunk(j, diag):
        kc = ik_ref[pl.ds(pl.multiple_of(j * tk, tk), tk), :]
        sc = jnp.zeros((tk, tq), F32)
        for hh in range(IDX_HEADS):
            sc = sc + wt[hh:hh + 1, :] * jnp.maximum(_dot_nt(kc, qms[hh]), 0.0)
        bits = pltpu.bitcast(sc, jnp.int32)
        key = bits ^ ((bits >> 31) & jnp.int32(0x7FFFFFFF))
        if diag:
            key = jnp.where(diag_keep, key, min_key)
        key_sc[j] = key

    def score_body(j, cr):
        score_chunk(j, False)
        return cr

    lax.fori_loop(0, nfull, score_body, 0)
    score_chunk(nfull, True)

    def count(pred_fn):
        def body(j, acc):
            hits = jnp.where(pred_fn(key_sc[j], j), 1.0, 0.0)
            parts = [hits[r:r + SUBLANES, :] for r in range(0, tk, SUBLANES)]
            while len(parts) > 1:
                parts = [a + b for a, b in zip(parts[::2], parts[1::2])]
            return acc + parts[0]
        acc = lax.fori_loop(0, nch, body, jnp.zeros((SUBLANES, tq), F32))
        return jnp.sum(acc, axis=0, keepdims=True)

    kf = float(topk)
    c_pos = count(lambda key, j: key >= 0)
    thr = jnp.where(c_pos >= kf, jnp.int32(0), min_key)
    cnt_ge = jnp.where(c_pos >= kf, c_pos, 0.0)

    def bit_cond(state):
        bi, _, cnt_ge = state
        return (bi < 31) & (jnp.min(jnp.where(cnt_ge == kf, 1.0, 0.0)) < 0.5)

    def bit_body(state):
        bi, thr, cnt_ge = state
        cand = thr | (jnp.int32(1) << (30 - bi))
        cnt = count(lambda key, j: key >= cand)
        take = cnt >= kf
        return bi + 1, jnp.where(take, cand, thr), jnp.where(take, cnt, cnt_ge)

    _, thr, cnt_ge = lax.while_loop(bit_cond, bit_body, (jnp.int32(0), thr, cnt_ge))
    thr = jnp.maximum(thr, min_key + 1)

    @pl.when(jnp.max(cnt_ge) > kf)
    def _():
        need = kf - count(lambda key, j: key > thr)

        def pos_body(bi, cut):
            cand = cut | (jnp.int32(1) << (30 - bi))
            cnt = count(lambda key, j: (key == thr) & (key_pos + j * tk < cand))
            return jnp.where(cnt < need, cand, cut)

        cut = lax.fori_loop(0, 31, pos_body, jnp.zeros((1, tq), jnp.int32))

        def demote(j, cr):
            key = key_sc[j]
            key_sc[j] = jnp.where((key == thr) & (key_pos + j * tk > cut), thr - 1, key)
            return cr

        lax.fori_loop(0, nch, demote, 0)

    c = DSA_HEAD_DIM ** -0.5 * LOG2E

    def logits(item, first):
        j = jnp.minimum(item, nfull)
        kc = k_ref[pl.ds(pl.multiple_of(j * tk, tk), tk), :]
        sel = key_sc[j] >= thr
        return tuple(jnp.where(sel, _dot_nt(kc, q_ref[:, hh * LANES:(hh + 1) * LANES]), NEG)
                     for hh in range(DSA_HEADS))

    _flash_chunks(nch, logits, lambda item, s: vt_ref[item], scratch, c)
    _, _, _, l_sc, acc_sc = scratch
    for hh in range(DSA_HEADS):
        o_ref[:, hh * LANES:(hh + 1) * LANES] = (acc_sc[hh] / l_sc[hh]).T.astype(o_ref.dtype)


def _dsa_attention(bq, bk, bvt, iq, ik, iwt, topk):
    bsz, seq, _ = bq.shape
    tq, tk = TQ, TM
    ratio = tk // tq
    kern = functools.partial(_dsa_kernel, tq=tq, tk=tk, topk=topk)
    full = lambda b, i: (b, 0, 0)
    tile = lambda b, i: (b, i, 0)
    return pl.pallas_call(
        kern,
        grid=(bsz, seq // tq),
        in_specs=[
            pl.BlockSpec((None, tq, B_Q), tile),
            pl.BlockSpec((None, seq, LANES), full),
            pl.BlockSpec((None, seq // tk, None, LANES, tk), lambda b, i: (b, 0, 0, 0, 0)),
            pl.BlockSpec((None, tq, I_Q), tile),
            pl.BlockSpec((None, seq, LANES), full),
            pl.BlockSpec((None, None, None, SUBLANES, tq), lambda b, i: (b, i // ratio, 0, 0, i % ratio)),
        ],
        out_specs=pl.BlockSpec((None, tq, B_Q), tile),
        out_shape=jax.ShapeDtypeStruct((bsz, seq, B_Q), BF16),
        scratch_shapes=[pltpu.VMEM((seq // tk, tk, tq), jnp.int32)] + _flash_scratch(DSA_HEADS, tk, tq, LANES),
        compiler_params=_cparams(("parallel", "arbitrary")),
        name="dsa",
    )(bq, bk, bvt, iq, ik, iwt)


def _ssd_kernel(xbc_ref, z_ref, dt_ref, cw_ref, cb_ref, dtb_ref, alog_ref, dskip_ref, ng_ref,
                e_pair_ref, e_full_ref, o_ref, xext_sc, state_sc, *, lc):
    ci = pl.program_id(1)
    npair = SSM_HEADS // 2
    gw = SSM_INNER // SSM_GROUPS

    @pl.when(ci == 0)
    def _():
        xext_sc[0:8, :] = jnp.zeros((8, SSM_CONV_DIM), F32)
        state_sc[...] = jnp.zeros(state_sc.shape, F32)

    xext_sc[8:8 + lc, :] = xbc_ref[...].astype(F32)
    conv = cb_ref[...] + cw_ref[SSM_CONV - 1:SSM_CONV, :] * xext_sc[8:8 + lc, :]
    for sh in range(1, SSM_CONV):
        conv = conv + cw_ref[SSM_CONV - 1 - sh:SSM_CONV - sh, :] * xext_sc[8 - sh:8 - sh + lc, :]
    xext_sc[0:8, :] = xext_sc[lc:lc + 8, :]
    xbc = conv * jax.nn.sigmoid(conv)
    xs = xbc[:, :SSM_INNER]
    bm = [xbc[:, SSM_INNER + g * SSM_STATE:SSM_INNER + (g + 1) * SSM_STATE] for g in range(SSM_GROUPS)]
    cm = [xbc[:, SSM_INNER + (SSM_GROUPS + g) * SSM_STATE:SSM_INNER + (SSM_GROUPS + g + 1) * SSM_STATE]
          for g in range(SSM_GROUPS)]

    dtr = dt_ref[...] + dtb_ref[...]
    dt = jnp.maximum(dtr, 0.0) + jnp.log1p(jnp.exp(-jnp.abs(dtr)))
    da = dt * (-jnp.exp(alog_ref[...]))
    r_i = lax.broadcasted_iota(jnp.int32, (lc, lc), 0)
    c_i = lax.broadcasted_iota(jnp.int32, (lc, lc), 1)
    tri = c_i <= r_i
    tri_bf = jnp.where(tri, 1.0, 0.0).astype(BF16)
    hi, mid, lo = _split3(da)
    acum = _dot(tri_bf, hi) + _dot(tri_bf, mid) + _dot(tri_bf, lo)
    acum_t = acum.T
    tail_t = jnp.exp(acum_t[:, lc - 1:lc] - acum_t)

    e_pair = e_pair_ref[...]
    e_full = e_full_ref[...]
    dt_x = _dot3(dt, e_pair)
    acum_x = _dot3(acum, e_pair)
    alast_x = acum_x[lc - 1:lc, :]
    acum_cb = _dot3(acum, e_full)

    xdt = (xs * dt_x).astype(BF16)
    lane = lax.broadcasted_iota(jnp.int32, (lc, LANES), 1)
    first = lane < SSM_HEAD_DIM
    lane_n = lax.broadcasted_iota(jnp.int32, (SSM_STATE, LANES), 1)
    first_n = lane_n < SSM_HEAD_DIM

    ys = []
    for g in range(SSM_GROUPS):
        cg = cm[g].astype(BF16)
        bg = bm[g].astype(BF16)
        cb = _dot_nt(cg, bg)
        bm_t = bm[g].T
        for pp in range(g * npair // SSM_GROUPS, (g + 1) * npair // SSM_GROUPS):
            xp = xdt[:, pp * LANES:(pp + 1) * LANES]
            yh, sh_new = [], []
            for sub in range(2):
                hd = 2 * pp + sub
                seg = acum_cb[:, hd * LANES:(hd + 1) * LANES] - acum_t[hd:hd + 1, :]
                decay = jnp.exp(jnp.where(tri, seg, -jnp.inf))
                yh.append(_dot((cb * decay).astype(BF16), xp))
                sh_new.append(_dot((bm_t * tail_t[hd:hd + 1, :]).astype(BF16), xp))
            st = state_sc[pp]
            y_in = _dot(cg, st.astype(BF16)) * jnp.exp(acum_x[:, pp * LANES:(pp + 1) * LANES])
            ys.append(jnp.where(first, yh[0], yh[1]) + y_in)
            state_sc[pp] = (st * jnp.exp(alast_x[:, pp * LANES:(pp + 1) * LANES])
                            + jnp.where(first_n, sh_new[0], sh_new[1]))
    y = jnp.concatenate(ys, axis=-1) + dskip_ref[...] * xs
    zf = z_ref[...].astype(F32)
    y = y * (zf * jax.nn.sigmoid(zf))
    for g in range(SSM_GROUPS):
        sl = slice(g * gw, (g + 1) * gw)
        o_ref[:, sl] = _rms(y[:, sl], ng_ref[:, sl]).astype(o_ref.dtype)


def _ssd(xbc, z, dt_raw, conv_w, conv_b, dt_bias, a_log, d_skip, norm_g, lc=128):
    bsz, seq, _ = xbc.shape
    pad = lambda v: jnp.pad(v.astype(F32), (0, LANES - v.shape[0])).reshape(1, LANES)
    heads = np.arange(LANES)[:, None]
    e_pair = jnp.asarray(heads == (np.arange(SSM_INNER)[None, :] // SSM_HEAD_DIM), BF16)
    e_full = jnp.asarray(heads == (np.arange(SSM_HEADS * LANES)[None, :] // LANES), BF16)
    dskip_x = jnp.repeat(d_skip.astype(F32), SSM_HEAD_DIM).reshape(1, SSM_INNER)
    tile = lambda b, c: (b, c, 0)
    const = lambda b, c: (0, 0)
    return pl.pallas_call(
        functools.partial(_ssd_kernel, lc=lc),
        grid=(bsz, seq // lc),
        in_specs=[
            pl.BlockSpec((None, lc, SSM_CONV_DIM), tile),
            pl.BlockSpec((None, lc, SSM_INNER), tile),
            pl.BlockSpec((None, lc, LANES), tile),
            pl.BlockSpec((SSM_CONV, SSM_CONV_DIM), const),
            pl.BlockSpec((1, SSM_CONV_DIM), const),
            pl.BlockSpec((1, LANES), const),
            pl.BlockSpec((1, LANES), const),
            pl.BlockSpec((1, SSM_INNER), const),
            pl.BlockSpec((1, SSM_INNER), const),
            pl.BlockSpec((LANES, SSM_INNER), const),
            pl.BlockSpec((LANES, SSM_HEADS * LANES), const),
        ],
        out_specs=pl.BlockSpec((None, lc, SSM_INNER), tile),
        out_shape=jax.ShapeDtypeStruct((bsz, seq, SSM_INNER), BF16),
        scratch_shapes=[
            pltpu.VMEM((lc + 8, SSM_CONV_DIM), F32),
            pltpu.VMEM((SSM_HEADS // 2, SSM_STATE, LANES), F32),
        ],
        compiler_params=_cparams(("parallel", "arbitrary")),
        name="ssd",
    )(xbc, z, dt_raw, conv_w.astype(F32), conv_b.astype(F32).reshape(1, SSM_CONV_DIM), pad(dt_bias),
      pad(a_log), dskip_x, norm_g.astype(F32).reshape(1, SSM_INNER), e_pair, e_full)


def _moba_kernel(q_ref, k_ref, vt_ref, o_ref, kmean_sc, sel_sc, *scratch, nblk, tk):
    qi = pl.program_id(2)
    tq = MOBA_BLOCK
    ratio = tk // tq
    hd = MOBA_HEAD_DIM
    c = hd ** -0.5 * LOG2E
    nrow = kmean_sc.shape[0]

    @pl.when(qi == 0)
    def _():
        kmean_sc[...] = jnp.zeros(kmean_sc.shape, F32)
        for j in range(nblk):
            kb = k_ref[j * tq:(j + 1) * tq, :].astype(F32)
            kmean_sc[j:j + 1, :] = jnp.sum(kb, axis=0, keepdims=True) * (1.0 / tq)

    qs = _pair_queries(q_ref[...])
    kmean = kmean_sc[...].astype(BF16)
    blk_id = lax.broadcasted_iota(jnp.int32, (nrow, tq), 0)
    blk_f = blk_id.astype(F32)
    causal = _tail_offsets(tq, tq) <= 0
    nfull = qi // ratio

    for sub in range(2):
        gate = jnp.where(blk_id < qi, _dot_nt(kmean, qs[sub]), -jnp.inf)
        sel = jnp.zeros((nrow, tq), F32)
        for _ in range(MOBA_TOPK):
            mx = jnp.max(gate, axis=0, keepdims=True)
            first = jnp.min(jnp.where(gate == mx, blk_f, float(nrow)), axis=0, keepdims=True)
            hit = (blk_f == first) & (mx > -jnp.inf)
            sel = jnp.where(hit, 1.0, sel)
            gate = jnp.where(hit, -jnp.inf, gate)
        sel_sc[sub] = sel

    def chunk_of(item):
        return jnp.where(item == 0, nfull, jnp.minimum(item - 1, nfull))

    def logits(item, first):
        j = chunk_of(item)
        kc = k_ref[pl.ds(pl.multiple_of(j * tk, tk), tk), :]
        sts = []
        for sub in range(2):
            st = _dot_nt(kc, qs[sub])
            parts = []
            for hb in range(ratio):
                blk = j * ratio + hb
                keep = sel_sc[sub, pl.ds(blk, 1), :] > 0.5
                if first:
                    keep = (keep & (blk < qi)) | (causal & (blk == qi))
                parts.append(jnp.where(keep, st[hb * tq:(hb + 1) * tq, :], NEG))
            sts.append(jnp.concatenate(parts, axis=0))
        return tuple(sts)

    def values(item, sub):
        return vt_ref[chunk_of(item)][sub * hd:(sub + 1) * hd, :]

    _flash_chunks(nfull + 1, logits, values, scratch, c)
    _, _, _, l_sc, acc_sc = scratch
    outs = [acc_sc[sub] / l_sc[sub] for sub in range(2)]
    o_ref[...] = jnp.concatenate(outs, axis=0).T.astype(o_ref.dtype)


def _moba_attention(mq, mk, mvt):
    bsz, seq, _ = mq.shape
    nblk = seq // MOBA_BLOCK
    tq, tk = MOBA_BLOCK, TM
    nrow = -(-nblk // 16) * 16
    return pl.pallas_call(
        functools.partial(_moba_kernel, nblk=nblk, tk=tk),
        grid=(bsz, MOBA_HEADS // 2, nblk),
        in_specs=[
            pl.BlockSpec((None, tq, LANES), lambda b, p, i: (b, i, p)),
            pl.BlockSpec((None, seq, LANES), lambda b, p, i: (b, 0, p)),
            pl.BlockSpec((None, seq // tk, None, LANES, tk), lambda b, p, i: (b, 0, p, 0, 0)),
        ],
        out_specs=pl.BlockSpec((None, tq, LANES), lambda b, p, i: (b, i, p)),
        out_shape=jax.ShapeDtypeStruct((bsz, seq, M_QKV), BF16),
        scratch_shapes=([pltpu.VMEM((nrow, LANES), F32), pltpu.VMEM((2, nrow, tq), F32)]
                        + _flash_scratch(2, tk, tq, MOBA_HEAD_DIM)),
        compiler_params=_cparams(("parallel", "parallel", "arbitrary")),
        name="moba",
    )(mq, mk, mvt)


_PERM64 = np.concatenate([np.arange(0, 32), np.arange(64, 96), np.arange(32, 64), np.arange(96, 128)])
_DUP64 = np.concatenate([np.arange(0, 32), np.arange(0, 32), np.arange(32, 64), np.arange(32, 64)])


def _perm_blocks(start, width):
    return np.concatenate([start + c + _PERM64 for c in range(0, width, LANES)])


def _rope_tables(seq):
    def tab(dim):
        inv = ROPE_THETA ** (-jnp.arange(0, dim, 2, dtype=F32) / dim)
        ang = jnp.arange(seq, dtype=F32)[:, None] * inv[None, :]
        return jnp.cos(ang), jnp.sin(ang)
    c64, s64 = tab(64)
    c128, s128 = tab(128)
    cos_t = jnp.stack([jnp.tile(c64, (1, 4)), jnp.tile(c128, (1, 2))])
    sin_t = jnp.stack([jnp.concatenate([-s64, -s64, s64, s64], axis=1),
                       jnp.concatenate([-s128, s128], axis=1)])
    return cos_t, sin_t


def _even_layout(pad_col):
    o = np.cumsum([0, A_QK, A_QK, A_V, B_Q, DSA_HEAD_DIM, DSA_HEAD_DIM, I_Q, IDX_HEAD_DIM, IDX_HEADS])
    pad = np.full(LANES - IDX_HEADS, pad_col)
    return [
        (_perm_blocks(o[0], A_QK), ROPE64, BF16, None),
        (_perm_blocks(o[1], A_QK), ROPE64, BF16, None),
        (np.arange(o[2], o[3]), TRANS, BF16, LANES),
        (np.arange(o[3], o[4]), ROPE128, BF16, None),
        (np.arange(o[4], o[5]), ROPE128, BF16, None),
        (np.arange(o[5], o[6]), TRANS, BF16, LANES),
        (_perm_blocks(o[6], I_Q), ROPE64, BF16, None),
        (o[7] + _DUP64, ROPE64, BF16, None),
        (np.concatenate([np.arange(o[8], o[9]), pad]), TRANS, F32, IDX_HEADS),
    ]


def _odd_layout(pad_col):
    o = np.cumsum([0, SSM_INNER, SSM_CONV_DIM, SSM_HEADS, M_QKV, M_QKV, M_QKV])
    pad = np.full(LANES - SSM_HEADS, pad_col)
    return [
        (np.arange(o[0], o[1]), PLAIN, BF16, None),
        (np.arange(o[1], o[2]), PLAIN, BF16, None),
        (np.concatenate([np.arange(o[2], o[3]), pad]), PLAIN, F32, None),
        (_perm_blocks(o[3], M_QKV), ROPE64, BF16, None),
        (_perm_blocks(o[4], M_QKV), ROPE64, BF16, None),
        (np.arange(o[5], o[6]), TRANS, BF16, LANES),
    ]


def _prep_in_weight(w, layout_fn):
    layout = layout_fn(w.shape[1])
    cols = np.concatenate([c for c, _, _, _ in layout])
    wl = jnp.concatenate([w, jnp.zeros((w.shape[0], 1), w.dtype)], axis=1)[:, cols].astype(BF16)
    widths = [len(c) for c, _, _, _ in layout]
    starts = np.concatenate([[0], np.cumsum(widths)[:-1]])
    groups = tuple((int(s), int(wd), int(k)) for s, wd, (_, k, _, _) in zip(starts, widths, layout))
    return wl, groups, [(dt, rows) for _, _, dt, rows in layout]


def kernel(x, norm_mix_pre, norm_mix_post, norm_ffn_pre, norm_ffn_post, ffn_gate, ffn_up, ffn_down,
           even_w_in, even_w_out, diff_lambda, diff_subln, odd_w_in, odd_w_out, ssm_conv_w,
           ssm_conv_b, ssm_dt_bias, ssm_a_log, ssm_d, ssm_norm):
    bsz, seq, d = x.shape
    depth = norm_mix_pre.shape[0]
    cos_t, sin_t = _rope_tables(seq)
    topk = min(DSA_TOPK_MAX, seq // 4)
    h = x.reshape(bsz * seq, d)

    def per_batch(t):
        if t.ndim == 2:
            return t.reshape(bsz, seq, t.shape[-1])
        return t.reshape(bsz, seq // TM, *t.shape[1:])

    for i in range(depth):
        j = i // 2
        if i % 2 == 0:
            w, groups, outs = _prep_in_weight(even_w_in[j], _even_layout)
            aq, ak, avt, bq, bk, bvt, iq, ik, iwt = map(
                per_batch, _inproj(h, norm_mix_pre[i], w, cos_t, sin_t, groups, outs, seq))
            lam_init = 0.8 - 0.6 * math.exp(-0.3 * i)
            lf = diff_lambda[j].astype(F32)
            lam = jnp.exp(jnp.sum(lf[0] * lf[1])) - jnp.exp(jnp.sum(lf[2] * lf[3])) + lam_init
            p1 = _diff_attention(aq, ak, avt, lam, diff_subln[j].astype(F32), lam_init)
            p2 = _dsa_attention(bq, bk, bvt, iq, ik, iwt, topk)
            w_out = even_w_out[j].astype(BF16)
        else:
            w, groups, outs = _prep_in_weight(odd_w_in[j], _odd_layout)
            z, xbc, dtr, mq, mk, mvt = map(
                per_batch, _inproj(h, norm_mix_pre[i], w, cos_t, sin_t, groups, outs, seq))
            p1 = _ssd(xbc, z, dtr, ssm_conv_w[j], ssm_conv_b[j], ssm_dt_bias[j], ssm_a_log[j], ssm_d[j],
                      ssm_norm[j])
            p2 = _moba_attention(mq, mk, mvt)
            w_out = odd_w_out[j].astype(BF16)
        k1 = p1.shape[-1]
        h = _outproj(h, p1.reshape(bsz * seq, k1), p2.reshape(bsz * seq, p2.shape[-1]),
                     w_out[:k1], w_out[k1:], norm_mix_post[i])
        h = _ffn(h, norm_ffn_pre[i], ffn_gate[i].astype(BF16), ffn_up[i].astype(BF16),
                 ffn_down[i].astype(BF16), norm_ffn_post[i])
    return h.reshape(bsz, seq, d)
```

```python
import functools
import math

import jax
import jax.numpy as jnp
import numpy as np
from jax import lax
from jax.experimental import pallas as pl
from jax.experimental.pallas import tpu as pltpu

F32 = jnp.float32
BF16 = jnp.bfloat16

D_MODEL = 1024
DIFF_HEADS = 4
DIFF_HEAD_DIM = 64
DSA_HEADS = 4
DSA_HEAD_DIM = 128
IDX_HEADS = 8
IDX_HEAD_DIM = 64
DSA_TOPK_MAX = 256
SSM_HEADS = 16
SSM_HEAD_DIM = 64
SSM_GROUPS = 2
SSM_STATE = 128
SSM_CONV = 4
SSM_INNER = SSM_HEADS * SSM_HEAD_DIM
SSM_CONV_DIM = SSM_INNER + 2 * SSM_GROUPS * SSM_STATE
MOBA_HEADS = 8
MOBA_HEAD_DIM = 64
MOBA_BLOCK = 256
MOBA_TOPK = 3
D_FF = 2816
ROPE_THETA = 10000.0
RMS_EPS = 1e-6

LANES = 128
SUBLANES = 8
PACKED_ROWS = 16
HALF16 = 1 << 15
VMEM_LIMIT = 56 * 1024 * 1024
NEG = -1e30
INT_MIN = -(2 ** 31)
LOG2E = math.log2(math.e)

TM = 512
TQ = 256
ROW_BLOCK = 64

A_QK = 2 * DIFF_HEADS * DIFF_HEAD_DIM
A_V = DIFF_HEADS * 2 * DIFF_HEAD_DIM
B_Q = DSA_HEADS * DSA_HEAD_DIM
I_Q = IDX_HEADS * IDX_HEAD_DIM
M_QKV = MOBA_HEADS * MOBA_HEAD_DIM

PLAIN, ROPE64, ROPE128, TRANS = 0, 1, 2, 3


def _cparams(sem):
    return pltpu.CompilerParams(dimension_semantics=sem, vmem_limit_bytes=VMEM_LIMIT)


def _rms(x, g):
    return x * lax.rsqrt(jnp.mean(x * x, axis=-1, keepdims=True) + RMS_EPS) * g


def _dot(a, b):
    return jnp.dot(a, b, preferred_element_type=F32)


def _dot_nt(a, b):
    return lax.dot_general(a, b, (((1,), (1,)), ((), ())), preferred_element_type=F32)


def _split3(x):
    hi = x.astype(BF16)
    r1 = x - hi.astype(F32)
    mid = r1.astype(BF16)
    lo = (r1 - mid.astype(F32)).astype(BF16)
    return hi, mid, lo


def _dot3(x, m):
    hi, mid, lo = _split3(x)
    return _dot(hi, m) + _dot(mid, m) + _dot(lo, m)


def _inproj_kernel(x_ref, g_ref, w_ref, cos_ref, sin_ref, *out_refs, groups):
    xn = _rms(x_ref[...], g_ref[...]).astype(BF16)
    for (start, width, kind), o_ref in zip(groups, out_refs):
        y = _dot(xn, w_ref[:, start:start + width])
        if kind == PLAIN:
            o_ref[...] = y.astype(o_ref.dtype)
        elif kind == TRANS:
            rows = o_ref.shape[1]
            for blk in range(width // LANES):
                o_ref[blk] = y[:, blk * LANES:(blk + 1) * LANES].T[:rows].astype(o_ref.dtype)
        else:
            cos = cos_ref[kind - 1]
            sin = sin_ref[kind - 1]
            for c in range(0, width, LANES):
                yb = y[:, c:c + LANES]
                o_ref[:, c:c + LANES] = (yb * cos + pltpu.roll(yb, 64, 1) * sin).astype(o_ref.dtype)


def _inproj(h, g, w, cos_t, sin_t, groups, outs, seq):
    t, d = h.shape
    n = w.shape[1]
    tm = TM
    per_seq = seq // tm
    kern = functools.partial(_inproj_kernel, groups=groups)
    out_shape, out_specs = [], []
    for (_, wd, kind), (dt, rows) in zip(groups, outs):
        if kind == TRANS:
            nb = wd // LANES
            out_shape.append(jax.ShapeDtypeStruct((t // tm, nb, rows, tm), dt))
            out_specs.append(pl.BlockSpec((None, nb, rows, tm), lambda i: (i, 0, 0, 0)))
        else:
            out_shape.append(jax.ShapeDtypeStruct((t, wd), dt))
            out_specs.append(pl.BlockSpec((tm, wd), lambda i: (i, 0)))
    return pl.pallas_call(
        kern,
        grid=(t // tm,),
        in_specs=[
            pl.BlockSpec((tm, d), lambda i: (i, 0)),
            pl.BlockSpec((1, d), lambda i: (0, 0)),
            pl.BlockSpec((d, n), lambda i: (0, 0)),
            pl.BlockSpec((2, tm, LANES), lambda i: (0, i % per_seq, 0)),
            pl.BlockSpec((2, tm, LANES), lambda i: (0, i % per_seq, 0)),
        ],
        out_specs=out_specs,
        out_shape=out_shape,
        compiler_params=_cparams(("parallel",)),
        name="inproj",
    )(h, g.reshape(1, d), w, cos_t, sin_t)


def _outproj_kernel(h_ref, a_ref, b_ref, wa_ref, wb_ref, g_ref, o_ref):
    y = _dot(a_ref[...], wa_ref[...]) + _dot(b_ref[...], wb_ref[...])
    o_ref[...] = h_ref[...] + _rms(y, g_ref[...])


def _outproj(h, a, b, wa, wb, g):
    t, d = h.shape
    tm = TM
    ka, kb = a.shape[1], b.shape[1]
    return pl.pallas_call(
        _outproj_kernel,
        grid=(t // tm,),
        in_specs=[
            pl.BlockSpec((tm, d), lambda i: (i, 0)),
            pl.BlockSpec((tm, ka), lambda i: (i, 0)),
            pl.BlockSpec((tm, kb), lambda i: (i, 0)),
            pl.BlockSpec((ka, d), lambda i: (0, 0)),
            pl.BlockSpec((kb, d), lambda i: (0, 0)),
            pl.BlockSpec((1, d), lambda i: (0, 0)),
        ],
        out_specs=pl.BlockSpec((tm, d), lambda i: (i, 0)),
        out_shape=jax.ShapeDtypeStruct((t, d), F32),
        compiler_params=_cparams(("parallel",)),
        name="outproj",
    )(h, a, b, wa, wb, g.reshape(1, d))


def _ffn_kernel(h_ref, gpre_ref, wg_ref, wu_ref, wd_ref, gpost_ref, o_ref, acc_ref, *, fc):
    h = h_ref[...]
    hn = _rms(h, gpre_ref[...]).astype(BF16)
    dff = wg_ref.shape[1]
    for c in range(0, dff, fc):
        gt = _dot(hn, wg_ref[:, c:c + fc])
        up = _dot(hn, wu_ref[:, c:c + fc])
        act = (gt * jax.nn.sigmoid(gt) * up).astype(BF16)
        part = _dot(act, wd_ref[c:c + fc, :])
        if c == 0:
            acc_ref[...] = part
        else:
            acc_ref[...] += part
    o_ref[...] = h + _rms(acc_ref[...], gpost_ref[...])


def _ffn(h, gpre, wg, wu, wd, gpost, fc=256):
    t, d = h.shape
    tm = TM
    dff = wg.shape[1]
    const = dict(pipeline_mode=pl.Buffered(1))
    return pl.pallas_call(
        functools.partial(_ffn_kernel, fc=fc),
        grid=(t // tm,),
        in_specs=[
            pl.BlockSpec((tm, d), lambda i: (i, 0)),
            pl.BlockSpec((1, d), lambda i: (0, 0)),
            pl.BlockSpec((d, dff), lambda i: (0, 0), **const),
            pl.BlockSpec((d, dff), lambda i: (0, 0), **const),
            pl.BlockSpec((dff, d), lambda i: (0, 0), **const),
            pl.BlockSpec((1, d), lambda i: (0, 0)),
        ],
        out_specs=pl.BlockSpec((tm, d), lambda i: (i, 0)),
        out_shape=jax.ShapeDtypeStruct((t, d), F32),
        scratch_shapes=[pltpu.VMEM((tm, d), F32)],
        compiler_params=_cparams(("parallel",)),
        name="ffn",
    )(h, gpre.reshape(1, d), wg, wu, wd, gpost.reshape(1, d))


def _online_t(st, vt, m, l, acc, c):
    m_new = jnp.maximum(m, jnp.max(st, axis=0, keepdims=True) * c)
    alpha = jnp.exp2(m - m_new)
    p = jnp.concatenate([jnp.exp2(st[r:r + ROW_BLOCK] * c - m_new).astype(BF16)
                         for r in range(0, st.shape[0], ROW_BLOCK)], axis=0)
    ones = jnp.ones((SUBLANES, p.shape[0]), BF16)
    l = alpha * l + _dot(ones, p)[:1]
    acc = alpha * acc + _dot(vt, p)
    return m_new, l, acc


def _pair_queries(q):
    lane = lax.broadcasted_iota(jnp.int32, q.shape, 1)
    is_a = (lane % 64) < 32
    zero = jnp.zeros_like(q)
    return jnp.where(is_a, q, zero), jnp.where(is_a, zero, q)


def _flash_chunks(n_items, logits_fn, values_fn, scratch, c):
    st_a, st_b, m_sc, l_sc, acc_sc = scratch
    m_sc[...] = jnp.full(m_sc.shape, NEG, F32)
    l_sc[...] = jnp.zeros(l_sc.shape, F32)
    acc_sc[...] = jnp.zeros(acc_sc.shape, F32)

    def fill(buf, item, first=False):
        for s, st in enumerate(logits_fn(item, first)):
            buf[s] = st

    def drain(buf, item):
        for s in range(buf.shape[0]):
            m, l, acc = _online_t(buf[s], values_fn(item, s), m_sc[s], l_sc[s], acc_sc[s], c)
            m_sc[s] = m
            l_sc[s] = l
            acc_sc[s] = acc

    fill(st_a, 0, first=True)

    def body(t, carry):
        fill(st_b, 2 * t + 1)
        drain(st_a, 2 * t)
        fill(st_a, 2 * t + 2)
        drain(st_b, 2 * t + 1)
        return carry

    lax.fori_loop(0, n_items // 2, body, 0)

    @pl.when(n_items % 2 == 1)
    def _():
        drain(st_a, n_items - 1)


def _flash_scratch(streams, tk, tq, dv):
    return [pltpu.VMEM((streams, tk, tq), F32), pltpu.VMEM((streams, tk, tq), F32),
            pltpu.VMEM((streams, 1, tq), F32), pltpu.VMEM((streams, 1, tq), F32),
            pltpu.VMEM((streams, dv, tq), F32)]


def _tail_offsets(tk, tq):
    return (lax.broadcasted_iota(jnp.int32, (tk, tq), 0)
            - lax.broadcasted_iota(jnp.int32, (tk, tq), 1))


def _diff_kernel(lam_ref, q_ref, k_ref, vt_ref, g_ref, o_ref, *scratch, tq, tk, out_scale):
    qi = pl.program_id(2)
    ratio = tk // tq
    nfull = qi // ratio
    qs = _pair_queries(q_ref[...])
    c = DIFF_HEAD_DIM ** -0.5 * LOG2E

    def chunk_of(item):
        return jnp.where(item == 0, nfull, jnp.minimum(item - 1, nfull))

    def logits(item, first):
        kc = k_ref[pl.ds(pl.multiple_of(chunk_of(item) * tk, tk), tk), :]
        sts = tuple(_dot_nt(kc, qm) for qm in qs)
        if first:
            diag_keep = _tail_offsets(tk, tq) <= tq * (qi % ratio)
            sts = tuple(jnp.where(diag_keep, st, NEG) for st in sts)
        return sts

    _flash_chunks(nfull + 1, logits, lambda item, s: vt_ref[chunk_of(item)], scratch, c)
    _, _, _, l_sc, acc_sc = scratch
    o = (acc_sc[0] / l_sc[0] - lam_ref[0] * (acc_sc[1] / l_sc[1])).T
    o_ref[...] = (_rms(o, g_ref[...]) * out_scale).astype(o_ref.dtype)


def _diff_attention(aq, ak, avt, lam, subln_g, lam_init):
    bsz, seq, _ = aq.shape
    tq, tk = TQ, TM
    kern = functools.partial(_diff_kernel, tq=tq, tk=tk, out_scale=1.0 - lam_init)
    return pl.pallas_call(
        kern,
        grid=(bsz, DIFF_HEADS, seq // tq),
        in_specs=[
            pl.BlockSpec(memory_space=pltpu.SMEM),
            pl.BlockSpec((None, tq, LANES), lambda b, h, i: (b, i, h)),
            pl.BlockSpec((None, seq, LANES), lambda b, h, i: (b, 0, h)),
            pl.BlockSpec((None, seq // tk, None, LANES, tk), lambda b, h, i: (b, 0, h, 0, 0)),
            pl.BlockSpec((1, LANES), lambda b, h, i: (0, 0)),
        ],
        out_specs=pl.BlockSpec((None, tq, LANES), lambda b, h, i: (b, i, h)),
        out_shape=jax.ShapeDtypeStruct((bsz, seq, A_V), BF16),
        scratch_shapes=_flash_scratch(2, tk, tq, LANES),
        compiler_params=_cparams(("parallel", "parallel", "arbitrary")),
        name="diff_attn",
    )(lam.reshape(1), aq, ak, avt, subln_g.reshape(1, LANES))


def _dsa_kernel(q_ref, k_ref, vt_ref, iq_ref, ik_ref, iwt_ref, o_ref, key_sc, hi_sc, lo_sc, *scratch,
                tq, tk, topk):
    qi = pl.program_id(1)
    ratio = tk // tq
    nfull = qi // ratio
    nch = nfull + 1
    diag_keep = _tail_offsets(tk, tq) <= tq * (qi % ratio)
    key_pos = lax.broadcasted_iota(jnp.int32, (tk, tq), 0)
    min_key = jnp.int32(INT_MIN)

    wt = iwt_ref[...] * (IDX_HEADS ** -0.5 * IDX_HEAD_DIM ** -0.5)
    qms = []
    for blk in range(IDX_HEADS // 2):
        qms.extend(_pair_queries(iq_ref[:, blk * LANES:(blk + 1) * LANES]))

    def score_chunk(j, diag):
        kc = ik_ref[pl.ds(pl.multiple_of(j * tk, tk), tk), :]
        sc = jnp.zeros((tk, tq), F32)
        for hh in range(IDX_HEADS):
            sc = sc + wt[hh:hh + 1, :] * jnp.maximum(_dot_nt(kc, qms[hh]), 0.0)
        bits = pltpu.bitcast(sc, jnp.int32)
        key = bits ^ ((bits >> 31) & jnp.int32(0x7FFFFFFF))
        if diag:
            key = jnp.where(diag_keep, key, min_key)
        key_sc[j] = key
        hi_sc[j] = (key >> 16).astype(jnp.int16)
        lo_sc[j] = ((key & 0xFFFF) - HALF16).astype(jnp.int16)

    def score_body(j, cr):
        score_chunk(j, False)
        return cr

    lax.fori_loop(0, nfull, score_body, 0)
    score_chunk(nfull, True)

    def count16(ref, cand):
        c16 = cand.astype(jnp.int16)

        def body(j, acc):
            hits = jnp.where(ref[j] >= c16, jnp.int16(1), jnp.int16(0))
            parts = [hits[r:r + PACKED_ROWS, :] for r in range(0, tk, PACKED_ROWS)]
            while len(parts) > 1:
                parts = [a + b for a, b in zip(parts[::2], parts[1::2])]
            return acc + parts[0]

        acc = lax.fori_loop(0, nch, body, jnp.zeros((PACKED_ROWS, tq), jnp.int16))
        return jnp.sum(acc.astype(jnp.int32), axis=0, keepdims=True)

    def search16(ref, want):
        c_pos = count16(ref, jnp.zeros((1, tq), jnp.int32))
        take = c_pos >= want
        state = (jnp.where(take, 0, -HALF16), jnp.where(take, c_pos, 0), jnp.where(take, 0, c_pos))

        def bit_body(bi, state):
            t, cnt_ge, cnt_gt = state
            cand = t | (jnp.int32(1) << (14 - bi))
            cnt = count16(ref, cand)
            take = cnt >= want
            return jnp.where(take, cand, t), jnp.where(take, cnt, cnt_ge), jnp.where(take, cnt_gt, cnt)

        return lax.fori_loop(0, 15, bit_body, state)

    t_hi, ge_hi, gt_hi = search16(hi_sc, jnp.full((1, tq), topk, jnp.int32))
    t_hi16 = t_hi.astype(jnp.int16)

    def band_body(j, cr):
        lo_sc[j] = jnp.where(hi_sc[j] == t_hi16, lo_sc[j], jnp.int16(-HALF16))
        return cr

    lax.fori_loop(0, nch, band_body, 0)
    t_lo, ge_lo, _ = search16(lo_sc, topk - gt_hi)
    thr = jnp.maximum((t_hi << 16) + (t_lo + HALF16), min_key + 1)
    band_ge = jnp.where(ge_lo > 0, ge_lo, ge_hi - gt_hi)
    cnt_ge = jnp.where(ge_hi > 0, gt_hi + band_ge, 0).astype(F32)
    kf = float(topk)

    def count(pred_fn):
        def body(j, acc):
            hits = jnp.where(pred_fn(key_sc[j], j), 1.0, 0.0)
            parts = [hits[r:r + SUBLANES, :] for r in range(0, tk, SUBLANES)]
            while len(parts) > 1:
                parts = [a + b for a, b in zip(parts[::2], parts[1::2])]
            return acc + parts[0]
        acc = lax.fori_loop(0, nch, body, jnp.zeros((SUBLANES, tq), F32))
        return jnp.sum(acc, axis=0, keepdims=True)

    @pl.when(jnp.max(cnt_ge) > kf)
    def _():
        need = kf - count(lambda key, j: key > thr)

        def pos_body(bi, cut):
            cand = cut | (jnp.int32(1) << (30 - bi))
            cnt = count(lambda key, j: (key == thr) & (key_pos + j * tk < cand))
            return jnp.where(cnt < need, cand, cut)

        cut = lax.fori_loop(0, 31, pos_body, jnp.zeros((1, tq), jnp.int32))

        def demote(j, cr):
            key = key_sc[j]
            key_sc[j] = jnp.where((key == thr) & (key_pos + j * tk > cut), thr - 1, key)
            return cr

        lax.fori_loop(0, nch, demote, 0)

    c = DSA_HEAD_DIM ** -0.5 * LOG2E

    def logits(item, first):
        j = jnp.minimum(item, nfull)
        kc = k_ref[pl.ds(pl.multiple_of(j * tk, tk), tk), :]
        sel = key_sc[j] >= thr
        return tuple(jnp.where(sel, _dot_nt(kc, q_ref[:, hh * LANES:(hh + 1) * LANES]), NEG)
                     for hh in range(DSA_HEADS))

    _flash_chunks(nch, logits, lambda item, s: vt_ref[item], scratch, c)
    _, _, _, l_sc, acc_sc = scratch
    for hh in range(DSA_HEADS):
        o_ref[:, hh * LANES:(hh + 1) * LANES] = (acc_sc[hh] / l_sc[hh]).T.astype(o_ref.dtype)


def _dsa_attention(bq, bk, bvt, iq, ik, iwt, topk):
    bsz, seq, _ = bq.shape
    tq, tk = TQ, TM
    ratio = tk // tq
    kern = functools.partial(_dsa_kernel, tq=tq, tk=tk, topk=topk)
    full = lambda b, i: (b, 0, 0)
    tile = lambda b, i: (b, i, 0)
    return pl.pallas_call(
        kern,
        grid=(bsz, seq // tq),
        in_specs=[
            pl.BlockSpec((None, tq, B_Q), tile),
            pl.BlockSpec((None, seq, LANES), full),
            pl.BlockSpec((None, seq // tk, None, LANES, tk), lambda b, i: (b, 0, 0, 0, 0)),
            pl.BlockSpec((None, tq, I_Q), tile),
            pl.BlockSpec((None, seq, LANES), full),
            pl.BlockSpec((None, None, None, SUBLANES, tq), lambda b, i: (b, i // ratio, 0, 0, i % ratio)),
        ],
        out_specs=pl.BlockSpec((None, tq, B_Q), tile),
        out_shape=jax.ShapeDtypeStruct((bsz, seq, B_Q), BF16),
        scratch_shapes=([pltpu.VMEM((seq // tk, tk, tq), jnp.int32), pltpu.VMEM((seq // tk, tk, tq), jnp.int16),
                         pltpu.VMEM((seq // tk, tk, tq), jnp.int16)]
                        + _flash_scratch(DSA_HEADS, tk, tq, LANES)),
        compiler_params=_cparams(("parallel", "arbitrary")),
        name="dsa",
    )(bq, bk, bvt, iq, ik, iwt)


def _ssd_kernel(xbc_ref, z_ref, dt_ref, cw_ref, cb_ref, dtb_ref, alog_ref, dskip_ref, ng_ref,
                e_pair_ref, e_full_ref, o_ref, xext_sc, state_sc, *, lc):
    ci = pl.program_id(1)
    npair = SSM_HEADS // 2
    gw = SSM_INNER // SSM_GROUPS

    @pl.when(ci == 0)
    def _():
        xext_sc[0:8, :] = jnp.zeros((8, SSM_CONV_DIM), F32)
        state_sc[...] = jnp.zeros(state_sc.shape, F32)

    xext_sc[8:8 + lc, :] = xbc_ref[...].astype(F32)
    conv = cb_ref[...] + cw_ref[SSM_CONV - 1:SSM_CONV, :] * xext_sc[8:8 + lc, :]
    for sh in range(1, SSM_CONV):
        conv = conv + cw_ref[SSM_CONV - 1 - sh:SSM_CONV - sh, :] * xext_sc[8 - sh:8 - sh + lc, :]
    xext_sc[0:8, :] = xext_sc[lc:lc + 8, :]
    xbc = conv * jax.nn.sigmoid(conv)
    xs = xbc[:, :SSM_INNER]
    bm = [xbc[:, SSM_INNER + g * SSM_STATE:SSM_INNER + (g + 1) * SSM_STATE] for g in range(SSM_GROUPS)]
    cm = [xbc[:, SSM_INNER + (SSM_GROUPS + g) * SSM_STATE:SSM_INNER + (SSM_GROUPS + g + 1) * SSM_STATE]
          for g in range(SSM_GROUPS)]

    dtr = dt_ref[...] + dtb_ref[...]
    dt = jnp.maximum(dtr, 0.0) + jnp.log1p(jnp.exp(-jnp.abs(dtr)))
    da = dt * (-jnp.exp(alog_ref[...]))
    r_i = lax.broadcasted_iota(jnp.int32, (lc, lc), 0)
    c_i = lax.broadcasted_iota(jnp.int32, (lc, lc), 1)
    tri = c_i <= r_i
    tri_bf = jnp.where(tri, 1.0, 0.0).astype(BF16)
    hi, mid, lo = _split3(da)
    acum = _dot(tri_bf, hi) + _dot(tri_bf, mid) + _dot(tri_bf, lo)
    acum_t = acum.T
    tail_t = jnp.exp(acum_t[:, lc - 1:lc] - acum_t)

    e_pair = e_pair_ref[...]
    e_full = e_full_ref[...]
    dt_x = _dot3(dt, e_pair)
    acum_x = _dot3(acum, e_pair)
    alast_x = acum_x[lc - 1:lc, :]
    acum_cb = _dot3(acum, e_full)

    xdt = (xs * dt_x).astype(BF16)
    lane = lax.broadcasted_iota(jnp.int32, (lc, LANES), 1)
    first = lane < SSM_HEAD_DIM
    lane_n = lax.broadcasted_iota(jnp.int32, (SSM_STATE, LANES), 1)
    first_n = lane_n < SSM_HEAD_DIM

    ys = []
    for g in range(SSM_GROUPS):
        cg = cm[g].astype(BF16)
        bg = bm[g].astype(BF16)
        cb = _dot_nt(cg, bg)
        bm_t = bm[g].T
        for pp in range(g * npair // SSM_GROUPS, (g + 1) * npair // SSM_GROUPS):
            xp = xdt[:, pp * LANES:(pp + 1) * LANES]
            yh, sh_new = [], []
            for sub in range(2):
                hd = 2 * pp + sub
                seg = acum_cb[:, hd * LANES:(hd + 1) * LANES] - acum_t[hd:hd + 1, :]
                decay = jnp.exp(jnp.where(tri, seg, -jnp.inf))
                yh.append(_dot((cb * decay).astype(BF16), xp))
                sh_new.append(_dot((bm_t * tail_t[hd:hd + 1, :]).astype(BF16), xp))
            st = state_sc[pp]
            y_in = _dot(cg, st.astype(BF16)) * jnp.exp(acum_x[:, pp * LANES:(pp + 1) * LANES])
            ys.append(jnp.where(first, yh[0], yh[1]) + y_in)
            state_sc[pp] = (st * jnp.exp(alast_x[:, pp * LANES:(pp + 1) * LANES])
                            + jnp.where(first_n, sh_new[0], sh_new[1]))
    y = jnp.concatenate(ys, axis=-1) + dskip_ref[...] * xs
    zf = z_ref[...].astype(F32)
    y = y * (zf * jax.nn.sigmoid(zf))
    for g in range(SSM_GROUPS):
        sl = slice(g * gw, (g + 1) * gw)
        o_ref[:, sl] = _rms(y[:, sl], ng_ref[:, sl]).astype(o_ref.dtype)


def _ssd(xbc, z, dt_raw, conv_w, conv_b, dt_bias, a_log, d_skip, norm_g, lc=128):
    bsz, seq, _ = xbc.shape
    pad = lambda v: jnp.pad(v.astype(F32), (0, LANES - v.shape[0])).reshape(1, LANES)
    heads = np.arange(LANES)[:, None]
    e_pair = jnp.asarray(heads == (np.arange(SSM_INNER)[None, :] // SSM_HEAD_DIM), BF16)
    e_full = jnp.asarray(heads == (np.arange(SSM_HEADS * LANES)[None, :] // LANES), BF16)
    dskip_x = jnp.repeat(d_skip.astype(F32), SSM_HEAD_DIM).reshape(1, SSM_INNER)
    tile = lambda b, c: (b, c, 0)
    const = lambda b, c: (0, 0)
    return pl.pallas_call(
        functools.partial(_ssd_kernel, lc=lc),
        grid=(bsz, seq // lc),
        in_specs=[
            pl.BlockSpec((None, lc, SSM_CONV_DIM), tile),
            pl.BlockSpec((None, lc, SSM_INNER), tile),
            pl.BlockSpec((None, lc, LANES), tile),
            pl.BlockSpec((SSM_CONV, SSM_CONV_DIM), const),
            pl.BlockSpec((1, SSM_CONV_DIM), const),
            pl.BlockSpec((1, LANES), const),
            pl.BlockSpec((1, LANES), const),
            pl.BlockSpec((1, SSM_INNER), const),
            pl.BlockSpec((1, SSM_INNER), const),
            pl.BlockSpec((LANES, SSM_INNER), const),
            pl.BlockSpec((LANES, SSM_HEADS * LANES), const),
        ],
        out_specs=pl.BlockSpec((None, lc, SSM_INNER), tile),
        out_shape=jax.ShapeDtypeStruct((bsz, seq, SSM_INNER), BF16),
        scratch_shapes=[
            pltpu.VMEM((lc + 8, SSM_CONV_DIM), F32),
            pltpu.VMEM((SSM_HEADS // 2, SSM_STATE, LANES), F32),
        ],
        compiler_params=_cparams(("parallel", "arbitrary")),
        name="ssd",
    )(xbc, z, dt_raw, conv_w.astype(F32), conv_b.astype(F32).reshape(1, SSM_CONV_DIM), pad(dt_bias),
      pad(a_log), dskip_x, norm_g.astype(F32).reshape(1, SSM_INNER), e_pair, e_full)


def _moba_kernel(q_ref, k_ref, vt_ref, o_ref, kmean_sc, sel_sc, *scratch, nblk, tk):
    qi = pl.program_id(2)
    tq = MOBA_BLOCK
    ratio = tk // tq
    hd = MOBA_HEAD_DIM
    c = hd ** -0.5 * LOG2E
    nrow = kmean_sc.shape[0]

    @pl.when(qi == 0)
    def _():
        kmean_sc[...] = jnp.zeros(kmean_sc.shape, F32)
        for j in range(nblk):
            kb = k_ref[j * tq:(j + 1) * tq, :].astype(F32)
            kmean_sc[j:j + 1, :] = jnp.sum(kb, axis=0, keepdims=True) * (1.0 / tq)

    qs = _pair_queries(q_ref[...])
    kmean = kmean_sc[...].astype(BF16)
    blk_id = lax.broadcasted_iota(jnp.int32, (nrow, tq), 0)
    blk_f = blk_id.astype(F32)
    causal = _tail_offsets(tq, tq) <= 0
    nfull = qi // ratio

    for sub in range(2):
        gate = jnp.where(blk_id < qi, _dot_nt(kmean, qs[sub]), -jnp.inf)
        sel = jnp.zeros((nrow, tq), F32)
        for _ in range(MOBA_TOPK):
            mx = jnp.max(gate, axis=0, keepdims=True)
            first = jnp.min(jnp.where(gate == mx, blk_f, float(nrow)), axis=0, keepdims=True)
            hit = (blk_f == first) & (mx > -jnp.inf)
            sel = jnp.where(hit, 1.0, sel)
            gate = jnp.where(hit, -jnp.inf, gate)
        sel_sc[sub] = sel

    def chunk_of(item):
        return jnp.where(item == 0, nfull, jnp.minimum(item - 1, nfull))

    def logits(item, first):
        j = chunk_of(item)
        kc = k_ref[pl.ds(pl.multiple_of(j * tk, tk), tk), :]
        sts = []
        for sub in range(2):
            st = _dot_nt(kc, qs[sub])
            parts = []
            for hb in range(ratio):
                blk = j * ratio + hb
                keep = sel_sc[sub, pl.ds(blk, 1), :] > 0.5
                if first:
                    keep = (keep & (blk < qi)) | (causal & (blk == qi))
                parts.append(jnp.where(keep, st[hb * tq:(hb + 1) * tq, :], NEG))
            sts.append(jnp.concatenate(parts, axis=0))
        return tuple(sts)

    def values(item, sub):
        return vt_ref[chunk_of(item)][sub * hd:(sub + 1) * hd, :]

    _flash_chunks(nfull + 1, logits, values, scratch, c)
    _, _, _, l_sc, acc_sc = scratch
    outs = [acc_sc[sub] / l_sc[sub] for sub in range(2)]
    o_ref[...] = jnp.concatenate(outs, axis=0).T.astype(o_ref.dtype)


def _moba_attention(mq, mk, mvt):
    bsz, seq, _ = mq.shape
    nblk = seq // MOBA_BLOCK
    tq, tk = MOBA_BLOCK, TM
    nrow = -(-nblk // 16) * 16
    return pl.pallas_call(
        functools.partial(_moba_kernel, nblk=nblk, tk=tk),
        grid=(bsz, MOBA_HEADS // 2, nblk),
        in_specs=[
            pl.BlockSpec((None, tq, LANES), lambda b, p, i: (b, i, p)),
            pl.BlockSpec((None, seq, LANES), lambda b, p, i: (b, 0, p)),
            pl.BlockSpec((None, seq // tk, None, LANES, tk), lambda b, p, i: (b, 0, p, 0, 0)),
        ],
        out_specs=pl.BlockSpec((None, tq, LANES), lambda b, p, i: (b, i, p)),
        out_shape=jax.ShapeDtypeStruct((bsz, seq, M_QKV), BF16),
        scratch_shapes=([pltpu.VMEM((nrow, LANES), F32), pltpu.VMEM((2, nrow, tq), F32)]
                        + _flash_scratch(2, tk, tq, MOBA_HEAD_DIM)),
        compiler_params=_cparams(("parallel", "parallel", "arbitrary")),
        name="moba",
    )(mq, mk, mvt)


_PERM64 = np.concatenate([np.arange(0, 32), np.arange(64, 96), np.arange(32, 64), np.arange(96, 128)])
_DUP64 = np.concatenate([np.arange(0, 32), np.arange(0, 32), np.arange(32, 64), np.arange(32, 64)])


def _perm_blocks(start, width):
    return np.concatenate([start + c + _PERM64 for c in range(0, width, LANES)])


def _rope_tables(seq):
    def tab(dim):
        inv = ROPE_THETA ** (-jnp.arange(0, dim, 2, dtype=F32) / dim)
        ang = jnp.arange(seq, dtype=F32)[:, None] * inv[None, :]
        return jnp.cos(ang), jnp.sin(ang)
    c64, s64 = tab(64)
    c128, s128 = tab(128)
    cos_t = jnp.stack([jnp.tile(c64, (1, 4)), jnp.tile(c128, (1, 2))])
    sin_t = jnp.stack([jnp.concatenate([-s64, -s64, s64, s64], axis=1),
                       jnp.concatenate([-s128, s128], axis=1)])
    return cos_t, sin_t


def _even_layout(pad_col):
    o = np.cumsum([0, A_QK, A_QK, A_V, B_Q, DSA_HEAD_DIM, DSA_HEAD_DIM, I_Q, IDX_HEAD_DIM, IDX_HEADS])
    pad = np.full(LANES - IDX_HEADS, pad_col)
    return [
        (_perm_blocks(o[0], A_QK), ROPE64, BF16, None),
        (_perm_blocks(o[1], A_QK), ROPE64, BF16, None),
        (np.arange(o[2], o[3]), TRANS, BF16, LANES),
        (np.arange(o[3], o[4]), ROPE128, BF16, None),
        (np.arange(o[4], o[5]), ROPE128, BF16, None),
        (np.arange(o[5], o[6]), TRANS, BF16, LANES),
        (_perm_blocks(o[6], I_Q), ROPE64, BF16, None),
        (o[7] + _DUP64, ROPE64, BF16, None),
        (np.concatenate([np.arange(o[8], o[9]), pad]), TRANS, F32, IDX_HEADS),
    ]


def _odd_layout(pad_col):
    o = np.cumsum([0, SSM_INNER, SSM_CONV_DIM, SSM_HEADS, M_QKV, M_QKV, M_QKV])
    pad = np.full(LANES - SSM_HEADS, pad_col)
    return [
        (np.arange(o[0], o[1]), PLAIN, BF16, None),
        (np.arange(o[1], o[2]), PLAIN, BF16, None),
        (np.concatenate([np.arange(o[2], o[3]), pad]), PLAIN, F32, None),
        (_perm_blocks(o[3], M_QKV), ROPE64, BF16, None),
        (_perm_blocks(o[4], M_QKV), ROPE64, BF16, None),
        (np.arange(o[5], o[6]), TRANS, BF16, LANES),
    ]


def _prep_in_weight(w, layout_fn):
    layout = layout_fn(w.shape[1])
    cols = np.concatenate([c for c, _, _, _ in layout])
    wl = jnp.concatenate([w, jnp.zeros((w.shape[0], 1), w.dtype)], axis=1)[:, cols].astype(BF16)
    widths = [len(c) for c, _, _, _ in layout]
    starts = np.concatenate([[0], np.cumsum(widths)[:-1]])
    groups = tuple((int(s), int(wd), int(k)) for s, wd, (_, k, _, _) in zip(starts, widths, layout))
    return wl, groups, [(dt, rows) for _, _, dt, rows in layout]


def kernel(x, norm_mix_pre, norm_mix_post, norm_ffn_pre, norm_ffn_post, ffn_gate, ffn_up, ffn_down,
           even_w_in, even_w_out, diff_lambda, diff_subln, odd_w_in, odd_w_out, ssm_conv_w,
           ssm_conv_b, ssm_dt_bias, ssm_a_log, ssm_d, ssm_norm):
    bsz, seq, d = x.shape
    depth = norm_mix_pre.shape[0]
    cos_t, sin_t = _rope_tables(seq)
    topk = min(DSA_TOPK_MAX, seq // 4)
    h = x.reshape(bsz * seq, d)

    def per_batch(t):
        if t.ndim == 2:
            return t.reshape(bsz, seq, t.shape[-1])
        return t.reshape(bsz, seq // TM, *t.shape[1:])

    for i in range(depth):
        j = i // 2
        if i % 2 == 0:
            w, groups, outs = _prep_in_weight(even_w_in[j], _even_layout)
            aq, ak, avt, bq, bk, bvt, iq, ik, iwt = map(
                per_batch, _inproj(h, norm_mix_pre[i], w, cos_t, sin_t, groups, outs, seq))
            lam_init = 0.8 - 0.6 * math.exp(-0.3 * i)
            lf = diff_lambda[j].astype(F32)
            lam = jnp.exp(jnp.sum(lf[0] * lf[1])) - jnp.exp(jnp.sum(lf[2] * lf[3])) + lam_init
            p1 = _diff_attention(aq, ak, avt, lam, diff_subln[j].astype(F32), lam_init)
            p2 = _dsa_attention(bq, bk, bvt, iq, ik, iwt, topk)
            w_out = even_w_out[j].astype(BF16)
        else:
            w, groups, outs = _prep_in_weight(odd_w_in[j], _odd_layout)
            z, xbc, dtr, mq, mk, mvt = map(
                per_batch, _inproj(h, norm_mix_pre[i], w, cos_t, sin_t, groups, outs, seq))
            p1 = _ssd(xbc, z, dtr, ssm_conv_w[j], ssm_conv_b[j], ssm_dt_bias[j], ssm_a_log[j], ssm_d[j],
                      ssm_norm[j])
            p2 = _moba_attention(mq, mk, mvt)
            w_out = odd_w_out[j].astype(BF16)
        k1 = p1.shape[-1]
        h = _outproj(h, p1.reshape(bsz * seq, k1), p2.reshape(bsz * seq, p2.shape[-1]),
                     w_out[:k1], w_out[k1:], norm_mix_post[i])
        h = _ffn(h, norm_ffn_pre[i], ffn_gate[i].astype(BF16), ffn_up[i].astype(BF16),
                 ffn_down[i].astype(BF16), norm_ffn_post[i])
    return h.reshape(bsz, seq, d)
```

```python
import functools
import math

import jax
import jax.numpy as jnp
import numpy as np
from jax import lax
from jax.experimental import pallas as pl
from jax.experimental.pallas import tpu as pltpu

F32 = jnp.float32
BF16 = jnp.bfloat16

D_MODEL = 1024
DIFF_HEADS = 4
DIFF_HEAD_DIM = 64
DSA_HEADS = 4
DSA_HEAD_DIM = 128
IDX_HEADS = 8
IDX_HEAD_DIM = 64
DSA_TOPK_MAX = 256
SSM_HEADS = 16
SSM_HEAD_DIM = 64
SSM_GROUPS = 2
SSM_STATE = 128
SSM_CONV = 4
SSM_INNER = SSM_HEADS * SSM_HEAD_DIM
SSM_CONV_DIM = SSM_INNER + 2 * SSM_GROUPS * SSM_STATE
MOBA_HEADS = 8
MOBA_HEAD_DIM = 64
MOBA_BLOCK = 256
MOBA_TOPK = 3
D_FF = 2816
ROPE_THETA = 10000.0
RMS_EPS = 1e-6

LANES = 128
SUBLANES = 8
PACKED_ROWS = 16
HALF16 = 1 << 15
VMEM_LIMIT = 56 * 1024 * 1024
NEG = -1e30
INT_MIN = -(2 ** 31)
LOG2E = math.log2(math.e)

TM = 512
TQ = 256
ROW_BLOCK = 64

A_QK = 2 * DIFF_HEADS * DIFF_HEAD_DIM
A_V = DIFF_HEADS * 2 * DIFF_HEAD_DIM
B_Q = DSA_HEADS * DSA_HEAD_DIM
I_Q = IDX_HEADS * IDX_HEAD_DIM
M_QKV = MOBA_HEADS * MOBA_HEAD_DIM

PLAIN, ROPE64, ROPE128, TRANS = 0, 1, 2, 3


def _cparams(sem):
    return pltpu.CompilerParams(dimension_semantics=sem, vmem_limit_bytes=VMEM_LIMIT)


def _rms(x, g):
    return x * lax.rsqrt(jnp.mean(x * x, axis=-1, keepdims=True) + RMS_EPS) * g


def _dot(a, b):
    return jnp.dot(a, b, preferred_element_type=F32)


def _dot_nt(a, b):
    return lax.dot_general(a, b, (((1,), (1,)), ((), ())), preferred_element_type=F32)


def _split3(x):
    hi = x.astype(BF16)
    r1 = x - hi.astype(F32)
    mid = r1.astype(BF16)
    lo = (r1 - mid.astype(F32)).astype(BF16)
    return hi, mid, lo


def _dot3(x, m):
    hi, mid, lo = _split3(x)
    return _dot(hi, m) + _dot(mid, m) + _dot(lo, m)


def _inproj_kernel(x_ref, g_ref, w_ref, cos_ref, sin_ref, *out_refs, groups):
    xn = _rms(x_ref[...], g_ref[...]).astype(BF16)
    for (start, width, kind), o_ref in zip(groups, out_refs):
        y = _dot(xn, w_ref[:, start:start + width])
        if kind == PLAIN:
            o_ref[...] = y.astype(o_ref.dtype)
        elif kind == TRANS:
            rows = o_ref.shape[1]
            for blk in range(width // LANES):
                o_ref[blk] = y[:, blk * LANES:(blk + 1) * LANES].T[:rows].astype(o_ref.dtype)
        else:
            cos = cos_ref[kind - 1]
            sin = sin_ref[kind - 1]
            for c in range(0, width, LANES):
                yb = y[:, c:c + LANES]
                o_ref[:, c:c + LANES] = (yb * cos + pltpu.roll(yb, 64, 1) * sin).astype(o_ref.dtype)


def _inproj(h, g, w, cos_t, sin_t, groups, outs, seq):
    t, d = h.shape
    n = w.shape[1]
    tm = TM
    per_seq = seq // tm
    kern = functools.partial(_inproj_kernel, groups=groups)
    out_shape, out_specs = [], []
    for (_, wd, kind), (dt, rows) in zip(groups, outs):
        if kind == TRANS:
            nb = wd // LANES
            out_shape.append(jax.ShapeDtypeStruct((t // tm, nb, rows, tm), dt))
            out_specs.append(pl.BlockSpec((None, nb, rows, tm), lambda i: (i, 0, 0, 0)))
        else:
            out_shape.append(jax.ShapeDtypeStruct((t, wd), dt))
            out_specs.append(pl.BlockSpec((tm, wd), lambda i: (i, 0)))
    return pl.pallas_call(
        kern,
        grid=(t // tm,),
        in_specs=[
            pl.BlockSpec((tm, d), lambda i: (i, 0)),
            pl.BlockSpec((1, d), lambda i: (0, 0)),
            pl.BlockSpec((d, n), lambda i: (0, 0)),
            pl.BlockSpec((2, tm, LANES), lambda i: (0, i % per_seq, 0)),
            pl.BlockSpec((2, tm, LANES), lambda i: (0, i % per_seq, 0)),
        ],
        out_specs=out_specs,
        out_shape=out_shape,
        compiler_params=_cparams(("parallel",)),
        name="inproj",
    )(h, g.reshape(1, d), w, cos_t, sin_t)


def _mix_ffn_kernel(h_ref, a_ref, b_ref, wa_ref, wb_ref, gmix_ref, gpre_ref, wg_ref, wu_ref, wd_ref,
                    gpost_ref, o_ref, acc_ref, *, fc):
    mix = _dot(a_ref[...], wa_ref[...]) + _dot(b_ref[...], wb_ref[...])
    h = h_ref[...] + _rms(mix, gmix_ref[...])
    hn = _rms(h, gpre_ref[...]).astype(BF16)
    dff = wg_ref.shape[1]
    for c in range(0, dff, fc):
        gt = _dot(hn, wg_ref[:, c:c + fc])
        up = _dot(hn, wu_ref[:, c:c + fc])
        act = (gt * jax.nn.sigmoid(gt) * up).astype(BF16)
        part = _dot(act, wd_ref[c:c + fc, :])
        if c == 0:
            acc_ref[...] = part
        else:
            acc_ref[...] += part
    o_ref[...] = h + _rms(acc_ref[...], gpost_ref[...])


def _mix_ffn(h, a, b, wa, wb, gmix, gpre, wg, wu, wd, gpost, fc=256):
    t, d = h.shape
    tm = TM
    ka, kb = a.shape[1], b.shape[1]
    dff = wg.shape[1]
    row = lambda i: (i, 0)
    const = lambda i: (0, 0)
    once = dict(pipeline_mode=pl.Buffered(1))
    vec = lambda v: v.reshape(1, d)
    return pl.pallas_call(
        functools.partial(_mix_ffn_kernel, fc=fc),
        grid=(t // tm,),
        in_specs=[
            pl.BlockSpec((tm, d), row),
            pl.BlockSpec((tm, ka), row),
            pl.BlockSpec((tm, kb), row),
            pl.BlockSpec((ka, d), const, **once),
            pl.BlockSpec((kb, d), const, **once),
            pl.BlockSpec((1, d), const),
            pl.BlockSpec((1, d), const),
            pl.BlockSpec((d, dff), const, **once),
            pl.BlockSpec((d, dff), const, **once),
            pl.BlockSpec((dff, d), const, **once),
            pl.BlockSpec((1, d), const),
        ],
        out_specs=pl.BlockSpec((tm, d), row),
        out_shape=jax.ShapeDtypeStruct((t, d), F32),
        scratch_shapes=[pltpu.VMEM((tm, d), F32)],
        compiler_params=_cparams(("parallel",)),
        name="mix_ffn",
    )(h, a, b, wa, wb, vec(gmix), vec(gpre), wg, wu, wd, vec(gpost))


def _online_t(st, vt, m, l, acc, c):
    m_new = jnp.maximum(m, jnp.max(st, axis=0, keepdims=True) * c)
    alpha = jnp.exp2(m - m_new)
    p = jnp.concatenate([jnp.exp2(st[r:r + ROW_BLOCK] * c - m_new).astype(BF16)
                         for r in range(0, st.shape[0], ROW_BLOCK)], axis=0)
    ones = jnp.ones((SUBLANES, p.shape[0]), BF16)
    l = alpha * l + _dot(ones, p)[:1]
    acc = alpha * acc + _dot(vt, p)
    return m_new, l, acc


def _pair_queries(q):
    lane = lax.broadcasted_iota(jnp.int32, q.shape, 1)
    is_a = (lane % 64) < 32
    zero = jnp.zeros_like(q)
    return jnp.where(is_a, q, zero), jnp.where(is_a, zero, q)


def _flash_chunks(n_items, logits_fn, values_fn, scratch, c):
    st_a, st_b, m_sc, l_sc, acc_sc = scratch
    m_sc[...] = jnp.full(m_sc.shape, NEG, F32)
    l_sc[...] = jnp.zeros(l_sc.shape, F32)
    acc_sc[...] = jnp.zeros(acc_sc.shape, F32)

    def fill(buf, item, first=False):
        for s, st in enumerate(logits_fn(item, first)):
            buf[s] = st

    def drain(buf, item):
        for s in range(buf.shape[0]):
            m, l, acc = _online_t(buf[s], values_fn(item, s), m_sc[s], l_sc[s], acc_sc[s], c)
            m_sc[s] = m
            l_sc[s] = l
            acc_sc[s] = acc

    fill(st_a, 0, first=True)

    def body(t, carry):
        fill(st_b, 2 * t + 1)
        drain(st_a, 2 * t)
        fill(st_a, 2 * t + 2)
        drain(st_b, 2 * t + 1)
        return carry

    lax.fori_loop(0, n_items // 2, body, 0)

    @pl.when(n_items % 2 == 1)
    def _():
        drain(st_a, n_items - 1)


def _flash_scratch(streams, tk, tq, dv):
    return [pltpu.VMEM((streams, tk, tq), F32), pltpu.VMEM((streams, tk, tq), F32),
            pltpu.VMEM((streams, 1, tq), F32), pltpu.VMEM((streams, 1, tq), F32),
            pltpu.VMEM((streams, dv, tq), F32)]


def _tail_offsets(tk, tq):
    return (lax.broadcasted_iota(jnp.int32, (tk, tq), 0)
            - lax.broadcasted_iota(jnp.int32, (tk, tq), 1))


def _diff_kernel(lam_ref, q_ref, k_ref, vt_ref, g_ref, o_ref, *scratch, tq, tk, out_scale):
    qi = pl.program_id(2)
    ratio = tk // tq
    nfull = qi // ratio
    qs = _pair_queries(q_ref[...])
    c = DIFF_HEAD_DIM ** -0.5 * LOG2E

    def chunk_of(item):
        return jnp.where(item == 0, nfull, jnp.minimum(item - 1, nfull))

    def logits(item, first):
        kc = k_ref[pl.ds(pl.multiple_of(chunk_of(item) * tk, tk), tk), :]
        sts = tuple(_dot_nt(kc, qm) for qm in qs)
        if first:
            diag_keep = _tail_offsets(tk, tq) <= tq * (qi % ratio)
            sts = tuple(jnp.where(diag_keep, st, NEG) for st in sts)
        return sts

    _flash_chunks(nfull + 1, logits, lambda item, s: vt_ref[chunk_of(item)], scratch, c)
    _, _, _, l_sc, acc_sc = scratch
    o = (acc_sc[0] / l_sc[0] - lam_ref[0] * (acc_sc[1] / l_sc[1])).T
    o_ref[...] = (_rms(o, g_ref[...]) * out_scale).astype(o_ref.dtype)


def _diff_attention(aq, ak, avt, lam, subln_g, lam_init):
    bsz, seq, _ = aq.shape
    tq, tk = TQ, TM
    kern = functools.partial(_diff_kernel, tq=tq, tk=tk, out_scale=1.0 - lam_init)
    return pl.pallas_call(
        kern,
        grid=(bsz, DIFF_HEADS, seq // tq),
        in_specs=[
            pl.BlockSpec(memory_space=pltpu.SMEM),
            pl.BlockSpec((None, tq, LANES), lambda b, h, i: (b, i, h)),
            pl.BlockSpec((None, seq, LANES), lambda b, h, i: (b, 0, h)),
            pl.BlockSpec((None, seq // tk, None, LANES, tk), lambda b, h, i: (b, 0, h, 0, 0)),
            pl.BlockSpec((1, LANES), lambda b, h, i: (0, 0)),
        ],
        out_specs=pl.BlockSpec((None, tq, LANES), lambda b, h, i: (b, i, h)),
        out_shape=jax.ShapeDtypeStruct((bsz, seq, A_V), BF16),
        scratch_shapes=_flash_scratch(2, tk, tq, LANES),
        compiler_params=_cparams(("parallel", "parallel", "arbitrary")),
        name="diff_attn",
    )(lam.reshape(1), aq, ak, avt, subln_g.reshape(1, LANES))


def _dsa_kernel(q_ref, k_ref, vt_ref, iq_ref, ik_ref, iwt_ref, o_ref, key_sc, hi_sc, lo_sc, *scratch,
                tq, tk, topk):
    qi = pl.program_id(1)
    ratio = tk // tq
    nfull = qi // ratio
    nch = nfull + 1
    diag_keep = _tail_offsets(tk, tq) <= tq * (qi % ratio)
    key_pos = lax.broadcasted_iota(jnp.int32, (tk, tq), 0)
    min_key = jnp.int32(INT_MIN)

    wt = iwt_ref[...] * (IDX_HEADS ** -0.5 * IDX_HEAD_DIM ** -0.5)
    qms = []
    for blk in range(IDX_HEADS // 2):
        qms.extend(_pair_queries(iq_ref[:, blk * LANES:(blk + 1) * LANES]))

    def score_chunk(j, diag):
        kc = ik_ref[pl.ds(pl.multiple_of(j * tk, tk), tk), :]
        sc = jnp.zeros((tk, tq), F32)
        for hh in range(IDX_HEADS):
            sc = sc + wt[hh:hh + 1, :] * jnp.maximum(_dot_nt(kc, qms[hh]), 0.0)
        bits = pltpu.bitcast(sc, jnp.int32)
        key = bits ^ ((bits >> 31) & jnp.int32(0x7FFFFFFF))
        if diag:
            key = jnp.where(diag_keep, key, min_key)
        key_sc[j] = key
        hi_sc[j] = (key >> 16).astype(jnp.int16)
        lo_sc[j] = ((key & 0xFFFF) - HALF16).astype(jnp.int16)

    def score_body(j, cr):
        score_chunk(j, False)
        return cr

    lax.fori_loop(0, nfull, score_body, 0)
    score_chunk(nfull, True)

    def count16(ref, cand):
        c16 = cand.astype(jnp.int16)

        def body(j, acc):
            hits = jnp.where(ref[j] >= c16, jnp.int16(1), jnp.int16(0))
            parts = [hits[r:r + PACKED_ROWS, :] for r in range(0, tk, PACKED_ROWS)]
            while len(parts) > 1:
                parts = [a + b for a, b in zip(parts[::2], parts[1::2])]
            return acc + parts[0]

        acc = lax.fori_loop(0, nch, body, jnp.zeros((PACKED_ROWS, tq), jnp.int16))
        return jnp.sum(acc.astype(jnp.int32), axis=0, keepdims=True)

    def search16(ref, want):
        c_pos = count16(ref, jnp.zeros((1, tq), jnp.int32))
        take = c_pos >= want
        state = (jnp.where(take, 0, -HALF16), jnp.where(take, c_pos, 0), jnp.where(take, 0, c_pos))

        def bit_body(bi, state):
            t, cnt_ge, cnt_gt = state
            cand = t | (jnp.int32(1) << (14 - bi))
            cnt = count16(ref, cand)
            take = cnt >= want
            return jnp.where(take, cand, t), jnp.where(take, cnt, cnt_ge), jnp.where(take, cnt_gt, cnt)

        return lax.fori_loop(0, 15, bit_body, state)

    t_hi, ge_hi, gt_hi = search16(hi_sc, jnp.full((1, tq), topk, jnp.int32))
    t_hi16 = t_hi.astype(jnp.int16)

    def band_body(j, cr):
        lo_sc[j] = jnp.where(hi_sc[j] == t_hi16, lo_sc[j], jnp.int16(-HALF16))
        return cr

    lax.fori_loop(0, nch, band_body, 0)
    t_lo, ge_lo, gt_lo = search16(lo_sc, topk - gt_hi)
    thr = jnp.maximum((t_hi << 16) + (t_lo + HALF16), min_key + 1)
    band_ge = jnp.where(ge_lo > 0, ge_lo, ge_hi - gt_hi)
    cnt_ge = jnp.where(ge_hi > 0, gt_hi + band_ge, 0)

    @pl.when(jnp.max(cnt_ge) > topk)
    def _():
        def rank_body(j, cr):
            pos = key_pos + j * tk
            lo_sc[j] = jnp.where(key_sc[j] == thr, -pos, -HALF16).astype(jnp.int16)
            return cr

        lax.fori_loop(0, nch, rank_body, 0)
        t_pos, _, _ = search16(lo_sc, topk - (gt_hi + gt_lo))

        def demote(j, cr):
            key = key_sc[j]
            key_sc[j] = jnp.where((key == thr) & (-(key_pos + j * tk) < t_pos), thr - 1, key)
            return cr

        lax.fori_loop(0, nch, demote, 0)

    c = DSA_HEAD_DIM ** -0.5 * LOG2E

    def logits(item, first):
        j = jnp.minimum(item, nfull)
        kc = k_ref[pl.ds(pl.multiple_of(j * tk, tk), tk), :]
        sel = key_sc[j] >= thr
        return tuple(jnp.where(sel, _dot_nt(kc, q_ref[:, hh * LANES:(hh + 1) * LANES]), NEG)
                     for hh in range(DSA_HEADS))

    _flash_chunks(nch, logits, lambda item, s: vt_ref[item], scratch, c)
    _, _, _, l_sc, acc_sc = scratch
    for hh in range(DSA_HEADS):
        o_ref[:, hh * LANES:(hh + 1) * LANES] = (acc_sc[hh] / l_sc[hh]).T.astype(o_ref.dtype)


def _dsa_attention(bq, bk, bvt, iq, ik, iwt, topk):
    bsz, seq, _ = bq.shape
    tq, tk = TQ, TM
    ratio = tk // tq
    kern = functools.partial(_dsa_kernel, tq=tq, tk=tk, topk=topk)
    full = lambda b, i: (b, 0, 0)
    tile = lambda b, i: (b, i, 0)
    return pl.pallas_call(
        kern,
        grid=(bsz, seq // tq),
        in_specs=[
            pl.BlockSpec((None, tq, B_Q), tile),
            pl.BlockSpec((None, seq, LANES), full),
            pl.BlockSpec((None, seq // tk, None, LANES, tk), lambda b, i: (b, 0, 0, 0, 0)),
            pl.BlockSpec((None, tq, I_Q), tile),
            pl.BlockSpec((None, seq, LANES), full),
            pl.BlockSpec((None, None, None, SUBLANES, tq), lambda b, i: (b, i // ratio, 0, 0, i % ratio)),
        ],
        out_specs=pl.BlockSpec((None, tq, B_Q), tile),
        out_shape=jax.ShapeDtypeStruct((bsz, seq, B_Q), BF16),
        scratch_shapes=([pltpu.VMEM((seq // tk, tk, tq), jnp.int32), pltpu.VMEM((seq // tk, tk, tq), jnp.int16),
                         pltpu.VMEM((seq // tk, tk, tq), jnp.int16)]
                        + _flash_scratch(DSA_HEADS, tk, tq, LANES)),
        compiler_params=_cparams(("parallel", "arbitrary")),
        name="dsa",
    )(bq, bk, bvt, iq, ik, iwt)


def _ssd_kernel(xbc_ref, z_ref, dt_ref, cw_ref, cb_ref, dtb_ref, alog_ref, dskip_ref, ng_ref,
                e_pair_ref, e_full_ref, o_ref, xext_sc, state_sc, *, lc):
    ci = pl.program_id(1)
    npair = SSM_HEADS // 2
    gw = SSM_INNER // SSM_GROUPS

    @pl.when(ci == 0)
    def _():
        xext_sc[0:8, :] = jnp.zeros((8, SSM_CONV_DIM), F32)
        state_sc[...] = jnp.zeros(state_sc.shape, F32)

    xext_sc[8:8 + lc, :] = xbc_ref[...].astype(F32)
    conv = cb_ref[...] + cw_ref[SSM_CONV - 1:SSM_CONV, :] * xext_sc[8:8 + lc, :]
    for sh in range(1, SSM_CONV):
        conv = conv + cw_ref[SSM_CONV - 1 - sh:SSM_CONV - sh, :] * xext_sc[8 - sh:8 - sh + lc, :]
    xext_sc[0:8, :] = xext_sc[lc:lc + 8, :]
    xbc = conv * jax.nn.sigmoid(conv)
    xs = xbc[:, :SSM_INNER]
    bm = [xbc[:, SSM_INNER + g * SSM_STATE:SSM_INNER + (g + 1) * SSM_STATE] for g in range(SSM_GROUPS)]
    cm = [xbc[:, SSM_INNER + (SSM_GROUPS + g) * SSM_STATE:SSM_INNER + (SSM_GROUPS + g + 1) * SSM_STATE]
          for g in range(SSM_GROUPS)]

    dtr = dt_ref[...] + dtb_ref[...]
    dt = jnp.maximum(dtr, 0.0) + jnp.log1p(jnp.exp(-jnp.abs(dtr)))
    da = dt * (-jnp.exp(alog_ref[...]))
    r_i = lax.broadcasted_iota(jnp.int32, (lc, lc), 0)
    c_i = lax.broadcasted_iota(jnp.int32, (lc, lc), 1)
    tri = c_i <= r_i
    tri_bf = jnp.where(tri, 1.0, 0.0).astype(BF16)
    hi, mid, lo = _split3(da)
    acum = _dot(tri_bf, hi) + _dot(tri_bf, mid) + _dot(tri_bf, lo)
    acum_t = acum.T
    tail_t = jnp.exp(acum_t[:, lc - 1:lc] - acum_t)

    e_pair = e_pair_ref[...]
    e_full = e_full_ref[...]
    dt_x = _dot3(dt, e_pair)
    acum_x = _dot3(acum, e_pair)
    alast_x = acum_x[lc - 1:lc, :]
    acum_cb = _dot3(acum, e_full)

    xdt = (xs * dt_x).astype(BF16)
    lane = lax.broadcasted_iota(jnp.int32, (lc, LANES), 1)
    first = lane < SSM_HEAD_DIM
    lane_n = lax.broadcasted_iota(jnp.int32, (SSM_STATE, LANES), 1)
    first_n = lane_n < SSM_HEAD_DIM

    ys = []
    for g in range(SSM_GROUPS):
        cg = cm[g].astype(BF16)
        bg = bm[g].astype(BF16)
        cb = _dot_nt(cg, bg)
        bm_t = bm[g].T
        for pp in range(g * npair // SSM_GROUPS, (g + 1) * npair // SSM_GROUPS):
            xp = xdt[:, pp * LANES:(pp + 1) * LANES]
            yh, sh_new = [], []
            for sub in range(2):
                hd = 2 * pp + sub
                seg = acum_cb[:, hd * LANES:(hd + 1) * LANES] - acum_t[hd:hd + 1, :]
                decay = jnp.exp(jnp.where(tri, seg, -jnp.inf))
                yh.append(_dot((cb * decay).astype(BF16), xp))
                sh_new.append(_dot((bm_t * tail_t[hd:hd + 1, :]).astype(BF16), xp))
            st = state_sc[pp]
            y_in = _dot(cg, st.astype(BF16)) * jnp.exp(acum_x[:, pp * LANES:(pp + 1) * LANES])
            ys.append(jnp.where(first, yh[0], yh[1]) + y_in)
            state_sc[pp] = (st * jnp.exp(alast_x[:, pp * LANES:(pp + 1) * LANES])
                            + jnp.where(first_n, sh_new[0], sh_new[1]))
    y = jnp.concatenate(ys, axis=-1) + dskip_ref[...] * xs
    zf = z_ref[...].astype(F32)
    y = y * (zf * jax.nn.sigmoid(zf))
    for g in range(SSM_GROUPS):
        sl = slice(g * gw, (g + 1) * gw)
        o_ref[:, sl] = _rms(y[:, sl], ng_ref[:, sl]).astype(o_ref.dtype)


def _ssd(xbc, z, dt_raw, conv_w, conv_b, dt_bias, a_log, d_skip, norm_g, lc=128):
    bsz, seq, _ = xbc.shape
    pad = lambda v: jnp.pad(v.astype(F32), (0, LANES - v.shape[0])).reshape(1, LANES)
    heads = np.arange(LANES)[:, None]
    e_pair = jnp.asarray(heads == (np.arange(SSM_INNER)[None, :] // SSM_HEAD_DIM), BF16)
    e_full = jnp.asarray(heads == (np.arange(SSM_HEADS * LANES)[None, :] // LANES), BF16)
    dskip_x = jnp.repeat(d_skip.astype(F32), SSM_HEAD_DIM).reshape(1, SSM_INNER)
    tile = lambda b, c: (b, c, 0)
    const = lambda b, c: (0, 0)
    return pl.pallas_call(
        functools.partial(_ssd_kernel, lc=lc),
        grid=(bsz, seq // lc),
        in_specs=[
            pl.BlockSpec((None, lc, SSM_CONV_DIM), tile),
            pl.BlockSpec((None, lc, SSM_INNER), tile),
            pl.BlockSpec((None, lc, LANES), tile),
            pl.BlockSpec((SSM_CONV, SSM_CONV_DIM), const),
            pl.BlockSpec((1, SSM_CONV_DIM), const),
            pl.BlockSpec((1, LANES), const),
            pl.BlockSpec((1, LANES), const),
            pl.BlockSpec((1, SSM_INNER), const),
            pl.BlockSpec((1, SSM_INNER), const),
            pl.BlockSpec((LANES, SSM_INNER), const),
            pl.BlockSpec((LANES, SSM_HEADS * LANES), const),
        ],
        out_specs=pl.BlockSpec((None, lc, SSM_INNER), tile),
        out_shape=jax.ShapeDtypeStruct((bsz, seq, SSM_INNER), BF16),
        scratch_shapes=[
            pltpu.VMEM((lc + 8, SSM_CONV_DIM), F32),
            pltpu.VMEM((SSM_HEADS // 2, SSM_STATE, LANES), F32),
        ],
        compiler_params=_cparams(("parallel", "arbitrary")),
        name="ssd",
    )(xbc, z, dt_raw, conv_w.astype(F32), conv_b.astype(F32).reshape(1, SSM_CONV_DIM), pad(dt_bias),
      pad(a_log), dskip_x, norm_g.astype(F32).reshape(1, SSM_INNER), e_pair, e_full)


def _moba_kernel(q_ref, k_ref, vt_ref, o_ref, kmean_sc, sel_sc, *scratch, nblk, tk):
    qi = pl.program_id(2)
    tq = MOBA_BLOCK
    ratio = tk // tq
    hd = MOBA_HEAD_DIM
    c = hd ** -0.5 * LOG2E
    nrow = kmean_sc.shape[0]

    @pl.when(qi == 0)
    def _():
        kmean_sc[...] = jnp.zeros(kmean_sc.shape, F32)
        for j in range(nblk):
            kb = k_ref[j * tq:(j + 1) * tq, :].astype(F32)
            kmean_sc[j:j + 1, :] = jnp.sum(kb, axis=0, keepdims=True) * (1.0 / tq)

    qs = _pair_queries(q_ref[...])
    kmean = kmean_sc[...].astype(BF16)
    blk_id = lax.broadcasted_iota(jnp.int32, (nrow, tq), 0)
    blk_f = blk_id.astype(F32)
    causal = _tail_offsets(tq, tq) <= 0
    nfull = qi // ratio

    for sub in range(2):
        gate = jnp.where(blk_id < qi, _dot_nt(kmean, qs[sub]), -jnp.inf)
        sel = jnp.zeros((nrow, tq), F32)
        for _ in range(MOBA_TOPK):
            mx = jnp.max(gate, axis=0, keepdims=True)
            first = jnp.min(jnp.where(gate == mx, blk_f, float(nrow)), axis=0, keepdims=True)
            hit = (blk_f == first) & (mx > -jnp.inf)
            sel = jnp.where(hit, 1.0, sel)
            gate = jnp.where(hit, -jnp.inf, gate)
        sel_sc[sub] = sel

    def chunk_of(item):
        return jnp.where(item == 0, nfull, jnp.minimum(item - 1, nfull))

    def logits(item, first):
        j = chunk_of(item)
        kc = k_ref[pl.ds(pl.multiple_of(j * tk, tk), tk), :]
        sts = []
        for sub in range(2):
            st = _dot_nt(kc, qs[sub])
            parts = []
            for hb in range(ratio):
                blk = j * ratio + hb
                keep = sel_sc[sub, pl.ds(blk, 1), :] > 0.5
                if first:
                    keep = (keep & (blk < qi)) | (causal & (blk == qi))
                parts.append(jnp.where(keep, st[hb * tq:(hb + 1) * tq, :], NEG))
            sts.append(jnp.concatenate(parts, axis=0))
        return tuple(sts)

    def values(item, sub):
        return vt_ref[chunk_of(item)][sub * hd:(sub + 1) * hd, :]

    _flash_chunks(nfull + 1, logits, values, scratch, c)
    _, _, _, l_sc, acc_sc = scratch
    outs = [acc_sc[sub] / l_sc[sub] for sub in range(2)]
    o_ref[...] = jnp.concatenate(outs, axis=0).T.astype(o_ref.dtype)


def _moba_attention(mq, mk, mvt):
    bsz, seq, _ = mq.shape
    nblk = seq // MOBA_BLOCK
    tq, tk = MOBA_BLOCK, TM
    nrow = -(-nblk // 16) * 16
    return pl.pallas_call(
        functools.partial(_moba_kernel, nblk=nblk, tk=tk),
        grid=(bsz, MOBA_HEADS // 2, nblk),
        in_specs=[
            pl.BlockSpec((None, tq, LANES), lambda b, p, i: (b, i, p)),
            pl.BlockSpec((None, seq, LANES), lambda b, p, i: (b, 0, p)),
            pl.BlockSpec((None, seq // tk, None, LANES, tk), lambda b, p, i: (b, 0, p, 0, 0)),
        ],
        out_specs=pl.BlockSpec((None, tq, LANES), lambda b, p, i: (b, i, p)),
        out_shape=jax.ShapeDtypeStruct((bsz, seq, M_QKV), BF16),
        scratch_shapes=([pltpu.VMEM((nrow, LANES), F32), pltpu.VMEM((2, nrow, tq), F32)]
                        + _flash_scratch(2, tk, tq, MOBA_HEAD_DIM)),
        compiler_params=_cparams(("parallel", "parallel", "arbitrary")),
        name="moba",
    )(mq, mk, mvt)


_PERM64 = np.concatenate([np.arange(0, 32), np.arange(64, 96), np.arange(32, 64), np.arange(96, 128)])
_DUP64 = np.concatenate([np.arange(0, 32), np.arange(0, 32), np.arange(32, 64), np.arange(32, 64)])


def _perm_blocks(start, width):
    return np.concatenate([start + c + _PERM64 for c in range(0, width, LANES)])


def _rope_tables(seq):
    def tab(dim):
        inv = ROPE_THETA ** (-jnp.arange(0, dim, 2, dtype=F32) / dim)
        ang = jnp.arange(seq, dtype=F32)[:, None] * inv[None, :]
        return jnp.cos(ang), jnp.sin(ang)
    c64, s64 = tab(64)
    c128, s128 = tab(128)
    cos_t = jnp.stack([jnp.tile(c64, (1, 4)), jnp.tile(c128, (1, 2))])
    sin_t = jnp.stack([jnp.concatenate([-s64, -s64, s64, s64], axis=1),
                       jnp.concatenate([-s128, s128], axis=1)])
    return cos_t, sin_t


def _even_layout(pad_col):
    o = np.cumsum([0, A_QK, A_QK, A_V, B_Q, DSA_HEAD_DIM, DSA_HEAD_DIM, I_Q, IDX_HEAD_DIM, IDX_HEADS])
    pad = np.full(LANES - IDX_HEADS, pad_col)
    return [
        (_perm_blocks(o[0], A_QK), ROPE64, BF16, None),
        (_perm_blocks(o[1], A_QK), ROPE64, BF16, None),
        (np.arange(o[2], o[3]), TRANS, BF16, LANES),
        (np.arange(o[3], o[4]), ROPE128, BF16, None),
        (np.arange(o[4], o[5]), ROPE128, BF16, None),
        (np.arange(o[5], o[6]), TRANS, BF16, LANES),
        (_perm_blocks(o[6], I_Q), ROPE64, BF16, None),
        (o[7] + _DUP64, ROPE64, BF16, None),
        (np.concatenate([np.arange(o[8], o[9]), pad]), TRANS, F32, IDX_HEADS),
    ]


def _odd_layout(pad_col):
    o = np.cumsum([0, SSM_INNER, SSM_CONV_DIM, SSM_HEADS, M_QKV, M_QKV, M_QKV])
    pad = np.full(LANES - SSM_HEADS, pad_col)
    return [
        (np.arange(o[0], o[1]), PLAIN, BF16, None),
        (np.arange(o[1], o[2]), PLAIN, BF16, None),
        (np.concatenate([np.arange(o[2], o[3]), pad]), PLAIN, F32, None),
        (_perm_blocks(o[3], M_QKV), ROPE64, BF16, None),
        (_perm_blocks(o[4], M_QKV), ROPE64, BF16, None),
        (np.arange(o[5], o[6]), TRANS, BF16, LANES),
    ]


def _prep_in_weight(w, layout_fn):
    layout = layout_fn(w.shape[1])
    cols = np.concatenate([c for c, _, _, _ in layout])
    wl = jnp.concatenate([w, jnp.zeros((w.shape[0], 1), w.dtype)], axis=1)[:, cols].astype(BF16)
    widths = [len(c) for c, _, _, _ in layout]
    starts = np.concatenate([[0], np.cumsum(widths)[:-1]])
    groups = tuple((int(s), int(wd), int(k)) for s, wd, (_, k, _, _) in zip(starts, widths, layout))
    return wl, groups, [(dt, rows) for _, _, dt, rows in layout]


def kernel(x, norm_mix_pre, norm_mix_post, norm_ffn_pre, norm_ffn_post, ffn_gate, ffn_up, ffn_down,
           even_w_in, even_w_out, diff_lambda, diff_subln, odd_w_in, odd_w_out, ssm_conv_w,
           ssm_conv_b, ssm_dt_bias, ssm_a_log, ssm_d, ssm_norm):
    bsz, seq, d = x.shape
    depth = norm_mix_pre.shape[0]
    cos_t, sin_t = _rope_tables(seq)
    topk = min(DSA_TOPK_MAX, seq // 4)
    h = x.reshape(bsz * seq, d)

    def per_batch(t):
        if t.ndim == 2:
            return t.reshape(bsz, seq, t.shape[-1])
        return t.reshape(bsz, seq // TM, *t.shape[1:])

    for i in range(depth):
        j = i // 2
        if i % 2 == 0:
            w, groups, outs = _prep_in_weight(even_w_in[j], _even_layout)
            aq, ak, avt, bq, bk, bvt, iq, ik, iwt = map(
                per_batch, _inproj(h, norm_mix_pre[i], w, cos_t, sin_t, groups, outs, seq))
            lam_init = 0.8 - 0.6 * math.exp(-0.3 * i)
            lf = diff_lambda[j].astype(F32)
            lam = jnp.exp(jnp.sum(lf[0] * lf[1])) - jnp.exp(jnp.sum(lf[2] * lf[3])) + lam_init
            p1 = _diff_attention(aq, ak, avt, lam, diff_subln[j].astype(F32), lam_init)
            p2 = _dsa_attention(bq, bk, bvt, iq, ik, iwt, topk)
            w_out = even_w_out[j].astype(BF16)
        else:
            w, groups, outs = _prep_in_weight(odd_w_in[j], _odd_layout)
            z, xbc, dtr, mq, mk, mvt = map(
                per_batch, _inproj(h, norm_mix_pre[i], w, cos_t, sin_t, groups, outs, seq))
            p1 = _ssd(xbc, z, dtr, ssm_conv_w[j], ssm_conv_b[j], ssm_dt_bias[j], ssm_a_log[j], ssm_d[j],
                      ssm_norm[j])
            p2 = _moba_attention(mq, mk, mvt)
            w_out = odd_w_out[j].astype(BF16)
        k1 = p1.shape[-1]
        h = _mix_ffn(h, p1.reshape(bsz * seq, k1), p2.reshape(bsz * seq, p2.shape[-1]), w_out[:k1], w_out[k1:],
                     norm_mix_post[i], norm_ffn_pre[i], ffn_gate[i].astype(BF16), ffn_up[i].astype(BF16),
                     ffn_down[i].astype(BF16), norm_ffn_post[i])
    return h.reshape(bsz, seq, d)
```
